```python
import math
import jax
import jax.numpy as jnp
from jax import lax
import numpy as np


D_MODEL = 2048
BATCH = 8
SEQ = 4096
DEPTH = 4

SB_HEADS = 16
SB_HEAD_DIM = 64
SB_WIDTH = SB_HEADS * SB_HEAD_DIM
Q_BLOCK = 128
RW_HEADS = 16
RW_HEAD_DIM = 64
RW_WIDTH = RW_HEADS * RW_HEAD_DIM
RW_DECAY_RANK = 96
RW_ICLR_RANK = 96
RW_VRES_RANK = 64
RW_GATE_RANK = 256
RW_GN_EPS = 64e-5
GDN_QK_HEADS = 16
GDN_V_HEADS = 32
GDN_HEAD_DIM = 128
GDN_KEY_WIDTH = GDN_QK_HEADS * GDN_HEAD_DIM
GDN_VAL_WIDTH = GDN_V_HEADS * GDN_HEAD_DIM
GDN_CONV = 4
GDN_CHUNK = 64
GDN_NORM_EPS = 1e-6
N_EXPERTS = 64
TOP_K = 8
N_GROUPS = 8
TOPK_GROUPS = 4
EXPERT_FF = 384
SHARED_FF = 384
ROUTED_SCALE = 2.5
DN_ALPHA = (2 * DEPTH) ** 0.25
DN_BETA = (8 * DEPTH) ** -0.25
LN_EPS = 1e-5
L2_EPS = 1e-6

kernel_name = 'hybrid_stickbreak_rwkv7_gdn_moe_deepnorm'


def _split_cols(t, sizes):
    bounds = [int(c) for c in np.cumsum(sizes)[:-1]]
    return jnp.split(t, bounds, axis=-1)


def _token_shift(t):
    return jnp.pad(t, ((0, 0), (1, 0), (0, 0)))[:, :-1]


def _layer_norm(x, g, b):
    xf = x.astype(jnp.float32)
    mean = jnp.mean(xf, axis=-1, keepdims=True)
    var = jnp.mean(jnp.square(xf - mean), axis=-1, keepdims=True)
    return ((xf - mean) * lax.rsqrt(var + LN_EPS) * g + b).astype(x.dtype)


def _l2_normalize(t):
    tf = t.astype(jnp.float32)
    return tf * lax.rsqrt(jnp.sum(tf * tf, axis=-1, keepdims=True) + L2_EPS)


def _swiglu(t, w_gate, w_up, w_down):
    return (jax.nn.silu(t @ w_gate) * (t @ w_up)) @ w_down


def _stick_breaking_attention(q, k, v):
    b, s, h, d = q.shape
    nb = s // Q_BLOCK
    qb = q.reshape(b, nb, Q_BLOCK, h, d).transpose(1, 0, 3, 2, 4)
    kh = k.transpose(0, 2, 1, 3)
    vh = v.transpose(0, 2, 1, 3)
    key_pos = jnp.arange(s)
    scale = d ** -0.5

    def block(args):
        q_i, blk = args
        z = jnp.einsum('bhqd,bhkd->bhqk', q_i, kh).astype(jnp.float32) * scale
        q_pos = blk * Q_BLOCK + jnp.arange(Q_BLOCK)
        causal = key_pos[None, :] < q_pos[:, None]
        log_keep = jnp.where(causal, -jax.nn.softplus(z), 0.0)
        later = lax.cumsum(log_keep, axis=3, reverse=True) - log_keep
        weights = jnp.where(causal, jnp.exp(jax.nn.log_sigmoid(z) + later), 0.0)
        return jnp.einsum('bhqk,bhkd->bhqd', weights.astype(v.dtype), vh)

    out = lax.map(block, (qb, jnp.arange(nb)))
    return out.transpose(1, 0, 3, 2, 4).reshape(b, s, h * d)


def _rwkv7_time_mix(pb, v_first, mu, w_up, w0, a_up, a0, v_up, v0, g_up,
                    k_k, k_a, r_k, lnx_g, lnx_b):
    b, s, _ = pb.shape
    pb = pb + (_token_shift(pb) - pb) * mu
    sizes = [RW_WIDTH, RW_WIDTH, RW_WIDTH, RW_DECAY_RANK, RW_ICLR_RANK, RW_GATE_RANK]
    if v_up is not None:
        sizes.append(RW_VRES_RANK)
    parts = _split_cols(pb, sizes)
    r, k, v, w_lo, a_lo, g_lo = parts[:6]
    w_log = -jax.nn.softplus(-(w0 + jnp.tanh(w_lo) @ w_up)) - 0.5
    decay = jnp.exp(-jnp.exp(w_log.astype(jnp.float32)))
    if v_up is None:
        v_first = v
    else:
        v = v + (v_first - v) * jax.nn.sigmoid(v0 + parts[6] @ v_up)
    a = jax.nn.sigmoid(a0 + a_lo @ a_up)
    g = jax.nn.sigmoid(g_lo) @ g_up

    def heads(t):
        return t.reshape(b, s, RW_HEADS, RW_HEAD_DIM)

    kk = _l2_normalize(heads(k * k_k))
    k = k * (1 + (a - 1) * k_a)
    r_h, k_h, v_h = heads(r), heads(k), heads(v)

    def step(state, inp):
        r_t, k_t, v_t, kk_t, a_t, w_t = inp
        sa = jnp.einsum('bhvk,bhk->bhv', state, -kk_t)
        state = (state * w_t[:, :, None, :]
                 + sa[..., None] * (kk_t * a_t)[:, :, None, :]
                 + v_t[..., None] * k_t[:, :, None, :])
        return state, jnp.einsum('bhvk,bhk->bhv', state, r_t)

    def time_major(t):
        return jnp.moveaxis(t.astype(jnp.float32), 1, 0)

    state0 = jnp.zeros((b, RW_HEADS, RW_HEAD_DIM, RW_HEAD_DIM), jnp.float32)
    xs = tuple(time_major(t) for t in (r_h, k_h, v_h, kk, heads(a), heads(decay)))
    _, y = lax.scan(step, state0, xs)
    y = jnp.moveaxis(y, 0, 1)
    mean = jnp.mean(y, axis=-1, keepdims=True)
    var = jnp.mean(jnp.square(y - mean), axis=-1, keepdims=True)
    y = ((y - mean) * lax.rsqrt(var + RW_GN_EPS)).reshape(b, s, RW_WIDTH) * lnx_g + lnx_b
    bonus = jnp.sum(r_h * k_h * r_k, axis=-1, keepdims=True) * v_h
    y = y + bonus.reshape(b, s, RW_WIDTH)
    return (y * g).astype(pb.dtype), v_first


def _even_mixer(x, v_first, w_in, mu, w_up, w0, a_up, a0, v_up, v0, g_up,
                k_k, k_a, r_k, lnx_g, lnx_b, w_out):
    b, s, _ = x.shape
    p = x @ w_in
    pa, pb = p[..., :3 * SB_WIDTH], p[..., 3 * SB_WIDTH:]
    q, k, v = [t.reshape(b, s, SB_HEADS, SB_HEAD_DIM) for t in jnp.split(pa, 3, axis=-1)]
    y_a = _stick_breaking_attention(q, k, v)
    y_b, v_first = _rwkv7_time_mix(pb, v_first, mu, w_up, w0, a_up, a0, v_up, v0, g_up,
                                   k_k, k_a, r_k, lnx_g, lnx_b)
    return jnp.concatenate([y_a, y_b], axis=-1) @ w_out, v_first


def _causal_depthwise_conv(t, w):
    return lax.conv_general_dilated(
        t, w[:, None, :].astype(t.dtype), window_strides=(1,),
        padding=[(GDN_CONV - 1, 0)], dimension_numbers=('NWC', 'WIO', 'NWC'),
        feature_group_count=t.shape[-1])


def _chunked_gated_delta_rule(q, k, v, g, beta):
    b, s, h, dk = q.shape
    dv = v.shape[-1]
    c = GDN_CHUNK
    n = s // c

    def chunks(t):
        t = t.astype(jnp.float32).reshape((b, n, c, h) + t.shape[3:])
        return jnp.moveaxis(t, 3, 1)

    q, k, v, g, beta = chunks(q), chunks(k), chunks(v), chunks(g), chunks(beta)
    g = jnp.cumsum(g, axis=-1)
    idx = jnp.arange(c)
    incl = idx[:, None] >= idx[None, :]
    strict = idx[:, None] > idx[None, :]
    decay = jnp.exp(jnp.where(incl, g[..., :, None] - g[..., None, :], -jnp.inf))
    kb = k * beta[..., None]
    low = jnp.where(strict, jnp.einsum('bhnid,bhnjd->bhnij', kb, k) * decay, 0.0)
    eye = jnp.eye(c, dtype=jnp.float32)
    t_inv = lax.linalg.triangular_solve(eye + low, jnp.broadcast_to(eye, low.shape),
                                        left_side=True, lower=True)
    u = jnp.einsum('bhnij,bhnjd->bhnid', t_inv, v * beta[..., None])
    w = jnp.einsum('bhnij,bhnjd->bhnid', t_inv, kb * jnp.exp(g)[..., None])
    attn = jnp.einsum('bhnid,bhnjd->bhnij', q, k) * decay
    q_dec = q * jnp.exp(g)[..., None]
    k_dec = k * jnp.exp(g[..., -1:] - g)[..., None]
    g_last = jnp.exp(g[..., -1])

    def step(state, inp):
        u_i, w_i, attn_i, q_i, k_i, gl_i = inp
        v_new = u_i - jnp.einsum('bhcd,bhde->bhce', w_i, state)
        o_i = (jnp.einsum('bhcd,bhde->bhce', q_i, state)
               + jnp.einsum('bhij,bhje->bhie', attn_i, v_new))
        state = state * gl_i[..., None, None] + jnp.einsum('bhcd,bhce->bhde', k_i, v_new)
        return state, o_i

    def chunk_major(t):
        return jnp.moveaxis(t, 2, 0)

    state0 = jnp.zeros((b, h, dk, dv), jnp.float32)
    xs = tuple(chunk_major(t) for t in (u, w, attn, q_dec, k_dec, g_last))
    _, o = lax.scan(step, state0, xs)
    return o.transpose(1, 0, 3, 2, 4).reshape(b, s, h, dv)


def _odd_mixer(x, w_in, conv_w, a_log, dt_bias, norm_g, w_out):
    b, s, _ = x.shape
    p = x @ w_in
    qkv, z, beta_in, a_in = _split_cols(
        p, [2 * GDN_KEY_WIDTH + GDN_VAL_WIDTH, GDN_VAL_WIDTH, GDN_V_HEADS, GDN_V_HEADS])
    qkv = jax.nn.silu(_causal_depthwise_conv(qkv, conv_w))
    q, k, v = _split_cols(qkv, [GDN_KEY_WIDTH, GDN_KEY_WIDTH, GDN_VAL_WIDTH])
    rep = GDN_V_HEADS // GDN_QK_HEADS
    q = jnp.repeat(_l2_normalize(q.reshape(b, s, GDN_QK_HEADS, GDN_HEAD_DIM)), rep, axis=2) * GDN_HEAD_DIM ** -0.5
    k = jnp.repeat(_l2_normalize(k.reshape(b, s, GDN_QK_HEADS, GDN_HEAD_DIM)), rep, axis=2)
    v = v.reshape(b, s, GDN_V_HEADS, GDN_HEAD_DIM)
    beta = jax.nn.sigmoid(beta_in.astype(jnp.float32))
    g = -jnp.exp(a_log.astype(jnp.float32)) * jax.nn.softplus((a_in + dt_bias).astype(jnp.float32))
    o = _chunked_gated_delta_rule(q, k, v, g, beta)
    z = z.reshape(b, s, GDN_V_HEADS, GDN_HEAD_DIM).astype(jnp.float32)
    o = o * lax.rsqrt(jnp.mean(o * o, axis=-1, keepdims=True) + GDN_NORM_EPS) * norm_g * jax.nn.silu(z)
    return o.reshape(b, s, GDN_VAL_WIDTH).astype(x.dtype) @ w_out


def _moe_ffn(h, router, router_bias, e_gate, e_up, e_down, s_gate, s_up, s_down):
    b, s, d = h.shape
    xt = h.reshape(b * s, d)
    n = xt.shape[0]
    scores = jax.nn.sigmoid((xt @ router).astype(jnp.float32))
    choice = scores + router_bias.astype(jnp.float32)
    per_group = N_EXPERTS // N_GROUPS
    group_score = lax.top_k(choice.reshape(n, N_GROUPS, per_group), 2)[0].sum(-1)
    _, top_groups = lax.top_k(group_score, TOPK_GROUPS)
    group_mask = jax.nn.one_hot(top_groups, N_GROUPS, dtype=jnp.float32).sum(1)
    expert_mask = jnp.repeat(group_mask, per_group, axis=1) > 0
    _, top_idx = lax.top_k(jnp.where(expert_mask, choice, -jnp.inf), TOP_K)
    top_w = jnp.take_along_axis(scores, top_idx, axis=1)
    top_w = top_w / jnp.sum(top_w, axis=-1, keepdims=True) * ROUTED_SCALE
    gates = jnp.einsum('nk,nke->ne', top_w,
                       jax.nn.one_hot(top_idx, N_EXPERTS, dtype=jnp.float32)).astype(h.dtype)
    y = _swiglu(xt, s_gate, s_up, s_down)
    for e in range(N_EXPERTS):
        y = y + gates[:, e:e + 1] * _swiglu(xt, e_gate[e], e_up[e], e_down[e])
    return y.reshape(b, s, d)


def _normal(key, shape, scale):
    return jax.random.normal(key, shape, jnp.float32) * scale


def _uniform(key, shape, lo, hi):
    return jax.random.uniform(key, shape, jnp.float32, lo, hi)


def setup_inputs(seed: int = 0) -> dict:
    key = jax.random.key(seed)
    keys = iter(jax.random.split(key, 40 * DEPTH + 1))
    out = {'x': _normal(next(keys), (BATCH, SEQ, D_MODEL), 1.0)}
    for i in range(DEPTH):
        p = 'l%d_' % i
        if i % 2 == 0:
            first = i == 0
            n_b = 3 * RW_WIDTH + RW_DECAY_RANK + RW_ICLR_RANK + RW_GATE_RANK + (0 if first else RW_VRES_RANK)
            col_scale = jnp.concatenate([
                jnp.ones((2 * SB_WIDTH,), jnp.float32), jnp.full((SB_WIDTH,), DN_BETA, jnp.float32),
                jnp.ones((2 * RW_WIDTH,), jnp.float32), jnp.full((RW_WIDTH,), DN_BETA, jnp.float32),
                jnp.ones((n_b - 3 * RW_WIDTH,), jnp.float32)])
            out[p + 'w_in'] = _normal(next(keys), (D_MODEL, 3 * SB_WIDTH + n_b), D_MODEL ** -0.5) * col_scale
            out[p + 'mu'] = _uniform(next(keys), (n_b,), 0.0, 1.0)
            out[p + 'w_up'] = _normal(next(keys), (RW_DECAY_RANK, RW_WIDTH), 0.5 * RW_DECAY_RANK ** -0.5)
            out[p + 'w0'] = _uniform(next(keys), (RW_WIDTH,), -6.5, -1.5)
            out[p + 'a_up'] = _normal(next(keys), (RW_ICLR_RANK, RW_WIDTH), 0.5 * RW_ICLR_RANK ** -0.5)
            out[p + 'a0'] = _normal(next(keys), (RW_WIDTH,), 0.1)
            if not first:
                out[p + 'v_up'] = _normal(next(keys), (RW_VRES_RANK, RW_WIDTH), 0.5 * RW_VRES_RANK ** -0.5)
                out[p + 'v0'] = 1.0 + _normal(next(keys), (RW_WIDTH,), 0.1)
            out[p + 'g_up'] = _normal(next(keys), (RW_GATE_RANK, RW_WIDTH), RW_GATE_RANK ** -0.5)
            out[p + 'k_k'] = 0.85 + _normal(next(keys), (RW_WIDTH,), 0.02)
            out[p + 'k_a'] = 1.0 + _normal(next(keys), (RW_WIDTH,), 0.02)
            out[p + 'r_k'] = _normal(next(keys), (RW_HEADS, RW_HEAD_DIM), 0.1)
            out[p + 'lnx_g'] = 1.0 + _normal(next(keys), (RW_WIDTH,), 0.02)
            out[p + 'lnx_b'] = _normal(next(keys), (RW_WIDTH,), 0.02)
            out[p + 'w_out'] = _normal(next(keys), (SB_WIDTH + RW_WIDTH, D_MODEL),
                                       DN_BETA * (SB_WIDTH + RW_WIDTH) ** -0.5)
        else:
            col_scale = jnp.concatenate([
                jnp.ones((2 * GDN_KEY_WIDTH,), jnp.float32), jnp.full((GDN_VAL_WIDTH,), DN_BETA, jnp.float32),
                jnp.ones((GDN_VAL_WIDTH + 2 * GDN_V_HEADS,), jnp.float32)])
            n_cols = 2 * GDN_KEY_WIDTH + 2 * GDN_VAL_WIDTH + 2 * GDN_V_HEADS
            out[p + 'w_in'] = _normal(next(keys), (D_MODEL, n_cols), D_MODEL ** -0.5) * col_scale
            out[p + 'conv'] = _normal(next(keys), (GDN_CONV, 2 * GDN_KEY_WIDTH + GDN_VAL_WIDTH), 0.5)
            out[p + 'a_log'] = jnp.log(_uniform(next(keys), (GDN_V_HEADS,), 1.0, 16.0))
            dt = jnp.exp(_uniform(next(keys), (GDN_V_HEADS,), math.log(1e-3), math.log(1e-1)))
            out[p + 'dt_bias'] = dt + jnp.log(-jnp.expm1(-dt))
            out[p + 'norm_g'] = 1.0 + _normal(next(keys), (GDN_HEAD_DIM,), 0.02)
            out[p + 'w_out'] = _normal(next(keys), (GDN_VAL_WIDTH, D_MODEL), DN_BETA * GDN_VAL_WIDTH ** -0.5)
        out[p + 'ln1_g'] = 1.0 + _normal(next(keys), (D_MODEL,), 0.02)
        out[p + 'ln1_b'] = _normal(next(keys), (D_MODEL,), 0.02)
        out[p + 'router'] = _normal(next(keys), (D_MODEL, N_EXPERTS), D_MODEL ** -0.5)
        out[p + 'router_bias'] = _normal(next(keys), (N_EXPERTS,), 0.01)
        out[p + 'e_gate'] = _normal(next(keys), (N_EXPERTS, D_MODEL, EXPERT_FF), D_MODEL ** -0.5)
        out[p + 'e_up'] = _normal(next(keys), (N_EXPERTS, D_MODEL, EXPERT_FF), D_MODEL ** -0.5)
        out[p + 'e_down'] = _normal(next(keys), (N_EXPERTS, EXPERT_FF, D_MODEL), DN_BETA * EXPERT_FF ** -0.5)
        out[p + 's_gate'] = _normal(next(keys), (D_MODEL, SHARED_FF), D_MODEL ** -0.5)
        out[p + 's_up'] = _normal(next(keys), (D_MODEL, SHARED_FF), D_MODEL ** -0.5)
        out[p + 's_down'] = _normal(next(keys), (SHARED_FF, D_MODEL), DN_BETA * SHARED_FF ** -0.5)
        out[p + 'ln2_g'] = 1.0 + _normal(next(keys), (D_MODEL,), 0.02)
        out[p + 'ln2_b'] = _normal(next(keys), (D_MODEL,), 0.02)
    return out


def reference(x,
              l0_w_in, l0_mu, l0_w_up, l0_w0, l0_a_up, l0_a0, l0_g_up, l0_k_k, l0_k_a, l0_r_k,
              l0_lnx_g, l0_lnx_b, l0_w_out,
              l0_ln1_g, l0_ln1_b, l0_router, l0_router_bias, l0_e_gate, l0_e_up, l0_e_down,
              l0_s_gate, l0_s_up, l0_s_down, l0_ln2_g, l0_ln2_b,
              l1_w_in, l1_conv, l1_a_log, l1_dt_bias, l1_norm_g, l1_w_out,
              l1_ln1_g, l1_ln1_b, l1_router, l1_router_bias, l1_e_gate, l1_e_up, l1_e_down,
              l1_s_gate, l1_s_up, l1_s_down, l1_ln2_g, l1_ln2_b,
              l2_w_in, l2_mu, l2_w_up, l2_w0, l2_a_up, l2_a0, l2_v_up, l2_v0, l2_g_up, l2_k_k, l2_k_a,
              l2_r_k, l2_lnx_g, l2_lnx_b, l2_w_out,
              l2_ln1_g, l2_ln1_b, l2_router, l2_router_bias, l2_e_gate, l2_e_up, l2_e_down,
              l2_s_gate, l2_s_up, l2_s_down, l2_ln2_g, l2_ln2_b,
              l3_w_in, l3_conv, l3_a_log, l3_dt_bias, l3_norm_g, l3_w_out,
              l3_ln1_g, l3_ln1_b, l3_router, l3_router_bias, l3_e_gate, l3_e_up, l3_e_down,
              l3_s_gate, l3_s_up, l3_s_down, l3_ln2_g, l3_ln2_b):
    mixer_params = [
        (l0_w_in, l0_mu, l0_w_up, l0_w0, l0_a_up, l0_a0, None, None, l0_g_up, l0_k_k, l0_k_a, l0_r_k,
         l0_lnx_g, l0_lnx_b, l0_w_out),
        (l1_w_in, l1_conv, l1_a_log, l1_dt_bias, l1_norm_g, l1_w_out),
        (l2_w_in, l2_mu, l2_w_up, l2_w0, l2_a_up, l2_a0, l2_v_up, l2_v0, l2_g_up, l2_k_k, l2_k_a, l2_r_k,
         l2_lnx_g, l2_lnx_b, l2_w_out),
        (l3_w_in, l3_conv, l3_a_log, l3_dt_bias, l3_norm_g, l3_w_out),
    ]
    ffn_params = [
        (l0_ln1_g, l0_ln1_b, l0_router, l0_router_bias, l0_e_gate, l0_e_up, l0_e_down,
         l0_s_gate, l0_s_up, l0_s_down, l0_ln2_g, l0_ln2_b),
        (l1_ln1_g, l1_ln1_b, l1_router, l1_router_bias, l1_e_gate, l1_e_up, l1_e_down,
         l1_s_gate, l1_s_up, l1_s_down, l1_ln2_g, l1_ln2_b),
        (l2_ln1_g, l2_ln1_b, l2_router, l2_router_bias, l2_e_gate, l2_e_up, l2_e_down,
         l2_s_gate, l2_s_up, l2_s_down, l2_ln2_g, l2_ln2_b),
        (l3_ln1_g, l3_ln1_b, l3_router, l3_router_bias, l3_e_gate, l3_e_up, l3_e_down,
         l3_s_gate, l3_s_up, l3_s_down, l3_ln2_g, l3_ln2_b),
    ]
    v_first = None
    for i in range(DEPTH):
        if i % 2 == 0:
            mix, v_first = _even_mixer(x, v_first, *mixer_params[i])
        else:
            mix = _odd_mixer(x, *mixer_params[i])
        (ln1_g, ln1_b, router, router_bias, e_gate, e_up, e_down,
         s_gate, s_up, s_down, ln2_g, ln2_b) = ffn_params[i]
        h = _layer_norm(DN_ALPHA * x + mix, ln1_g, ln1_b)
        x = _layer_norm(DN_ALPHA * h + _moe_ffn(h, router, router_bias, e_gate, e_up, e_down,
                                                  s_gate, s_up, s_down), ln2_g, ln2_b)
    return x
```

```python
import functools
import math

import jax
import jax.numpy as jnp
from jax import lax
from jax.experimental import pallas as pl
from jax.experimental.pallas import tpu as pltpu

D_MODEL = 2048
DEPTH = 4
SB_HEADS = 16
SB_HEAD_DIM = 64
SB_WIDTH = SB_HEADS * SB_HEAD_DIM
RW_HEADS = 16
RW_HEAD_DIM = 64
RW_WIDTH = RW_HEADS * RW_HEAD_DIM
RW_DECAY_RANK = 96
RW_ICLR_RANK = 96
RW_VRES_RANK = 64
RW_GATE_RANK = 256
RW_GN_EPS = 64e-5
GDN_QK_HEADS = 16
GDN_V_HEADS = 32
GDN_HEAD_DIM = 128
GDN_KEY_WIDTH = GDN_QK_HEADS * GDN_HEAD_DIM
GDN_VAL_WIDTH = GDN_V_HEADS * GDN_HEAD_DIM
GDN_CONV = 4
GDN_NORM_EPS = 1e-6
N_EXPERTS = 64
TOP_K = 8
N_GROUPS = 8
TOPK_GROUPS = 4
EXPERT_FF = 384
SHARED_FF = 384
ROUTED_SCALE = 2.5
DN_ALPHA = (2 * DEPTH) ** 0.25
LN_EPS = 1e-5
L2_EPS = 1e-6

LANES = 128
VMEM_LIMIT = 56 * 1024 * 1024
F32 = jnp.float32
BF16 = jnp.bfloat16
HI = lax.Precision.HIGHEST


def _params(*sem):
    return pltpu.CompilerParams(dimension_semantics=sem, vmem_limit_bytes=VMEM_LIMIT)


def _dot(a, b, hp=False):
    if hp:
        return jnp.dot(a.astype(F32), b.astype(F32), precision=HI, preferred_element_type=F32)
    return jnp.dot(a.astype(BF16), b.astype(BF16), preferred_element_type=F32)


def _dot_nt(a, b, hp=False):
    dims = (((1,), (1,)), ((), ()))
    if hp:
        return lax.dot_general(a.astype(F32), b.astype(F32), dims, precision=HI, preferred_element_type=F32)
    return lax.dot_general(a.astype(BF16), b.astype(BF16), dims, preferred_element_type=F32)


def _dot_tn(a, b, hp=False):
    dims = (((0,), (0,)), ((), ()))
    if hp:
        return lax.dot_general(a.astype(F32), b.astype(F32), dims, precision=HI, preferred_element_type=F32)
    return lax.dot_general(a.astype(BF16), b.astype(BF16), dims, preferred_element_type=F32)


def _sigmoid(x):
    return 1.0 / (1.0 + jnp.exp(-x))


def _softplus(x):
    return jnp.maximum(x, 0.0) + jnp.log(1.0 + jnp.exp(-jnp.abs(x)))


def _silu(x):
    return x * _sigmoid(x)


def _matmul_kernel(x_ref, w_ref, o_ref):
    o_ref[...] = jnp.dot(x_ref[...], w_ref[...], preferred_element_type=F32).astype(o_ref.dtype)


def _matmul(x, w, out_dtype, tm=1024, tn=1024):
    n, k = x.shape
    m = w.shape[1]
    tm = min(tm, n)
    tn = min(tn, m)
    assert n % tm == 0 and m % tn == 0, (n, m, tm, tn)
    return pl.pallas_call(
        _matmul_kernel,
        grid=(m // tn, n // tm),
        in_specs=[pl.BlockSpec((tm, k), lambda j, i: (i, 0)),
                  pl.BlockSpec((k, tn), lambda j, i: (0, j))],
        out_specs=pl.BlockSpec((tm, tn), lambda j, i: (i, j)),
        out_shape=jax.ShapeDtypeStruct((n, m), out_dtype),
        compiler_params=_params("arbitrary", "arbitrary"),
        name="matmul",
    )(x, w)


def _ln_rows(y, g, b):
    mean = jnp.mean(y, axis=-1, keepdims=True)
    yc = y - mean
    var = jnp.mean(yc * yc, axis=-1, keepdims=True)
    return yc * lax.rsqrt(var + LN_EPS) * g + b


def _add_ln_kernel(x_ref, m_ref, g_ref, b_ref, o_ref, ob_ref):
    y = DN_ALPHA * x_ref[...] + m_ref[...].astype(F32)
    out = _ln_rows(y, g_ref[...], b_ref[...])
    o_ref[...] = out
    ob_ref[...] = out.astype(BF16)


def _add_ln(x, mix, g, b, tm=512):
    n, d = x.shape
    tm = min(tm, n)
    row = pl.BlockSpec((tm, d), lambda i: (i, 0))
    vec = pl.BlockSpec((1, d), lambda i: (0, 0))
    return pl.pallas_call(
        _add_ln_kernel,
        grid=(n // tm,),
        in_specs=[row, row, vec, vec],
        out_specs=[row, row],
        out_shape=[jax.ShapeDtypeStruct((n, d), F32), jax.ShapeDtypeStruct((n, d), BF16)],
        compiler_params=_params("arbitrary"),
        name="add_ln",
    )(x, mix, g.reshape(1, d), b.reshape(1, d))


SB_TQ = 256
SB_TK = 128


def _sb_kernel(q_ref, k_ref, v_ref, o_ref, *, tq, tk):
    i = pl.program_id(2)
    q = q_ref[0] * jnp.asarray(SB_HEAD_DIM ** -0.5, BF16)
    lane = lax.broadcasted_iota(jnp.int32, (tq, LANES), 1)
    zero = jnp.zeros_like(q)
    q_heads = (jnp.where(lane < SB_HEAD_DIM, q, zero), jnp.where(lane >= SB_HEAD_DIM, q, zero))
    q_pos = i * tq + lax.broadcasted_iota(jnp.int32, (tq, tk), 0)
    k_off = lax.broadcasted_iota(jnp.int32, (tq, tk), 1)
    r2 = lax.broadcasted_iota(jnp.int32, (2 * tk, tk + LANES), 0) % tk
    c2 = lax.broadcasted_iota(jnp.int32, (2 * tk, tk + LANES), 1)
    cum = jnp.where((r2 > c2) | (c2 >= tk), 1.0, 0.0).astype(BF16)
    n_blocks = ((i + 1) * tq) // tk

    def body(jj, carry):
        j = n_blocks - 1 - jj
        start = pl.multiple_of(j * tk, tk)
        ks = k_ref[0, pl.ds(start, tk), :]
        vs = v_ref[0, pl.ds(start, tk), :]
        causal = (start + k_off) < q_pos
        new = []
        for h in range(2):
            acc, later_blocks = carry[2 * h], carry[2 * h + 1]
            z = _dot_nt(q_heads[h], ks)
            sp = _softplus(z)
            log_keep = jnp.where(causal, -sp, 0.0)
            hi = log_keep.astype(BF16)
            lo = (log_keep - hi.astype(F32)).astype(BF16)
            sums = jnp.dot(jnp.concatenate([hi, lo], axis=1), cum, preferred_element_type=F32)
            later = sums[:, :tk] + jnp.tile(later_blocks, (1, tk // LANES))
            w = jnp.where(causal, jnp.exp(z - sp + later), 0.0)
            acc = acc + jnp.dot(w.astype(BF16), vs, preferred_element_type=F32)
            new += [acc, later_blocks + sums[:, tk:]]
        return tuple(new)

    init = tuple(jnp.zeros((tq, LANES), F32) for _ in range(4))
    res = lax.fori_loop(0, n_blocks, body, init)
    o_ref[0] = jnp.where(lane < SB_HEAD_DIM, res[0], res[2]).astype(o_ref.dtype)


def _stick_breaking(qkv, b, s):
    tq = min(SB_TQ, s)
    tk = min(SB_TK, s)
    pairs = SB_WIDTH // LANES
    return pl.pallas_call(
        functools.partial(_sb_kernel, tq=tq, tk=tk),
        grid=(b, pairs, s // tq),
        in_specs=[pl.BlockSpec((1, tq, LANES), lambda bi, p, i: (bi, i, p)),
                  pl.BlockSpec((1, s, LANES), lambda bi, p, i: (bi, 0, pairs + p)),
                  pl.BlockSpec((1, s, LANES), lambda bi, p, i: (bi, 0, 2 * pairs + p))],
        out_specs=pl.BlockSpec((1, tq, LANES), lambda bi, p, i: (bi, i, p)),
        out_shape=jax.ShapeDtypeStruct((b, s, SB_WIDTH), BF16),
        compiler_params=_params("arbitrary", "arbitrary", "arbitrary"),
        name="stick_breaking",
    )(qkv, qkv, qkv)


def _split3(x):
    hi = x.astype(BF16)
    r1 = x - hi.astype(F32)
    mid = r1.astype(BF16)
    lo = (r1 - mid.astype(F32)).astype(BF16)
    return hi, mid, lo


def _dot_sel_rhs(x, sel):
    return sum(jnp.dot(p, sel, preferred_element_type=F32) for p in _split3(x))


def _dot_sel_lhs(sel, x):
    return sum(jnp.dot(sel, p, preferred_element_type=F32) for p in _split3(x))


def _head_sum_matrices(width, head_dim):
    heads = width // head_dim
    assert heads <= LANES
    c = lax.broadcasted_iota(jnp.int32, (width, LANES), 0) // head_dim
    h = lax.broadcasted_iota(jnp.int32, (width, LANES), 1)
    gather = jnp.where(c == h, 1.0, 0.0).astype(BF16)
    ct = lax.broadcasted_iota(jnp.int32, (LANES, width), 1) // head_dim
    ht = lax.broadcasted_iota(jnp.int32, (LANES, width), 0)
    spread = jnp.where(ct == ht, 1.0, 0.0).astype(BF16)
    return gather, spread


def _per_head_sum(x, gather, spread):
    return _dot_sel_rhs(_dot_sel_rhs(x, gather), spread)


def _neumann_inverse(a, size):
    eye = jnp.where(lax.broadcasted_iota(jnp.int32, (size, size), 0)
                    == lax.broadcasted_iota(jnp.int32, (size, size), 1), 1.0, 0.0)
    inv = eye + a
    power = a
    span = 2
    while span < size:
        power = _dot(power, power, hp=True)
        inv = inv + _dot(inv, power, hp=True)
        span *= 2
    return inv


RW_LORA = 512
RW_COLS = 3 * RW_WIDTH + RW_LORA
RW_TS = 256
RW_CHUNK = 64


def _rwkv_prep_kernel(pb_ref, mu_ref, wup_ref, aup_ref, gup_ref, vup_ref, vec_ref, vfirst_ref,
                      r_ref, k_ref, v_ref, kk_ref, a_ref, lw_ref, g_ref, bonus_ref,
                      last_ref, *, ts, has_vres):
    @pl.when(pl.program_id(1) == 0)
    def _():
        last_ref[...] = jnp.zeros_like(last_ref)

    x = pb_ref[...]
    prev = pltpu.roll(x, 1, 0)
    first_row = lax.broadcasted_iota(jnp.int32, x.shape, 0) == 0
    prev = jnp.where(first_row, jnp.broadcast_to(last_ref[0:1, :], x.shape), prev)
    last_ref[0:1, :] = x[ts - 1:ts, :]
    x = x + (prev - x) * mu_ref[...]

    w = RW_WIDTH
    r, k, v, lora = x[:, :w], x[:, w:2 * w], x[:, 2 * w:3 * w], x[:, 3 * w:]
    w0, a0, v0, k_k, k_a, r_k = (vec_ref[i:i + 1, :] for i in range(6))
    w_log = -_softplus(-(w0 + _dot(jnp.tanh(lora), wup_ref[...]))) - 0.5
    lw_ref[...] = -jnp.exp(w_log)
    if has_vres:
        v = v + (vfirst_ref[...] - v) * _sigmoid(v0 + _dot(lora, vup_ref[...]))
    a = _sigmoid(a0 + _dot(lora, aup_ref[...]))
    g_ref[...] = _dot(_sigmoid(lora), gup_ref[...])
    gather, spread = _head_sum_matrices(w, RW_HEAD_DIM)
    kk = k * k_k
    kk = kk * lax.rsqrt(_per_head_sum(kk * kk, gather, spread) + L2_EPS)
    k = k * (1.0 + (a - 1.0) * k_a)
    bonus_ref[...] = _per_head_sum(r * k * r_k, gather, spread) * v
    r_ref[...] = r
    k_ref[...] = k
    v_ref[...] = v
    kk_ref[...] = kk
    a_ref[...] = a


def _rwkv_prep(pb, b, s, mu, wup, aup, gup, vup, vecs, v_first):
    n = b * s
    ts = min(RW_TS, s)
    nt = s // ts
    has_vres = v_first is not None
    if not has_vres:
        v_first = jnp.zeros((8, RW_WIDTH), F32)
        vf_spec = pl.BlockSpec((8, RW_WIDTH), lambda bi, i: (0, 0))
    else:
        vf_spec = pl.BlockSpec((ts, RW_WIDTH), lambda bi, i: (bi * nt + i, 0))
    row = lambda c: pl.BlockSpec((ts, c), lambda bi, i: (bi * nt + i, 0))
    full = lambda a: pl.BlockSpec(a.shape, lambda bi, i: (0, 0))
    outs = [jax.ShapeDtypeStruct((n, RW_WIDTH), F32)] * 8
    return pl.pallas_call(
        functools.partial(_rwkv_prep_kernel, ts=ts, has_vres=has_vres),
        grid=(b, nt),
        in_specs=[row(RW_COLS), full(mu), full(wup), full(aup), full(gup), full(vup), full(vecs), vf_spec],
        out_specs=[row(RW_WIDTH)] * 8,
        out_shape=outs,
        scratch_shapes=[pltpu.VMEM((8, RW_COLS), F32)],
        compiler_params=_params("arbitrary", "arbitrary"),
        name="rwkv_prep",
    )(pb, mu, wup, aup, gup, vup, vecs, v_first)


def _rwkv_chunk_kernel(r_ref, k_ref, v_ref, kk_ref, a_ref, lw_ref, g_ref, bonus_ref, lnx_ref,
                       o_ref, state_ref, *, c):
    @pl.when(pl.program_id(1) == 0)
    def _():
        state_ref[...] = jnp.zeros_like(state_ref)

    rr = lax.broadcasted_iota(jnp.int32, (c, c), 0)
    cc = lax.broadcasted_iota(jnp.int32, (c, c), 1)
    strict = rr > cc
    incl = rr >= cc
    tri = jnp.where(incl, 1.0, 0.0).astype(BF16)

    lw = lw_ref[...]
    cum = _dot_sel_lhs(tri, lw)
    total = cum[c - 1:c, :]
    kk = kk_ref[...]
    b_vec = kk * a_ref[...]
    dec_out = jnp.exp(total - cum)
    inv_cum = jnp.exp(-cum)
    r_in = r_ref[...] * jnp.exp(cum)
    a_in = -kk * jnp.exp(cum - lw)
    b_out = b_vec * inv_cum
    k_out = k_ref[...] * inv_cum
    b_end = b_vec * dec_out
    k_end = k_ref[...] * dec_out
    v_all = v_ref[...]
    p_end = jnp.exp(total)

    lane = lax.broadcasted_iota(jnp.int32, (c, LANES), 1)
    first = lane < RW_HEAD_DIM
    blk_r = lax.broadcasted_iota(jnp.int32, (LANES, LANES), 0)
    blk_c = lax.broadcasted_iota(jnp.int32, (LANES, LANES), 1)
    same_head = (blk_r // RW_HEAD_DIM) == (blk_c // RW_HEAD_DIM)
    diag = blk_r == blk_c

    ys = []
    for p in range(RW_WIDTH // LANES):
        sl = slice(p * LANES, (p + 1) * LANES)
        ai, ri, bo, ko, v = a_in[:, sl], r_in[:, sl], b_out[:, sl], k_out[:, sl], v_all[:, sl]
        per_head = []
        for h in range(2):
            keep = first if h == 0 else ~first
            ai_h = jnp.where(keep, ai, 0.0)
            ri_h = jnp.where(keep, ri, 0.0)
            a_ab = jnp.where(strict, _dot_nt(ai_h, bo, hp=True), 0.0)
            a_ak = jnp.where(strict, _dot_nt(ai_h, ko), 0.0)
            a_rb = jnp.where(incl, _dot_nt(ri_h, bo), 0.0)
            a_rk = jnp.where(incl, _dot_nt(ri_h, ko), 0.0)
            t_inv = _neumann_inverse(a_ab, c)
            w_t = _dot(t_inv, ai, hp=True)
            u0 = _dot(t_inv, _dot(a_ak, v))
            per_head.append((a_rb, a_rk, w_t, u0))
        w_t = jnp.where(first, per_head[0][2], per_head[1][2])
        u0 = jnp.where(first, per_head[0][3], per_head[1][3])
        q_hat = ri + jnp.where(first, _dot(per_head[0][0], w_t), _dot(per_head[1][0], w_t))
        y0 = jnp.where(first,
                       _dot(per_head[0][0], u0) + _dot(per_head[0][1], v),
                       _dot(per_head[1][0], u0) + _dot(per_head[1][1], v))
        be, ke = b_end[:, sl], k_end[:, sl]
        m_t = jnp.where(same_head, _dot_tn(be, w_t, hp=True), 0.0)
        m_t = m_t + jnp.where(diag, jnp.broadcast_to(p_end[:, sl], (LANES, LANES)), 0.0)
        n_t = jnp.where(same_head, _dot_tn(be, u0) + _dot_tn(ke, v), 0.0)
        st = state_ref[p]
        ys.append(_dot(q_hat, st) + y0)
        state_ref[p] = _dot(m_t, st, hp=True) + n_t
    y = jnp.concatenate(ys, axis=1)

    gather, spread = _head_sum_matrices(RW_WIDTH, RW_HEAD_DIM)
    mean = _per_head_sum(y, gather, spread) * (1.0 / RW_HEAD_DIM)
    yc = y - mean
    var = _per_head_sum(yc * yc, gather, spread) * (1.0 / RW_HEAD_DIM)
    y = yc * lax.rsqrt(var + RW_GN_EPS) * lnx_ref[0:1, :] + lnx_ref[1:2, :]
    o_ref[...] = ((y + bonus_ref[...]) * g_ref[...]).astype(o_ref.dtype)


def _rwkv_chunks(parts, lnx, b, s):
    c = min(RW_CHUNK, s)
    nc = s // c
    row = pl.BlockSpec((c, RW_WIDTH), lambda bi, i: (bi * nc + i, 0))
    return pl.pallas_call(
        functools.partial(_rwkv_chunk_kernel, c=c),
        grid=(b, nc),
        in_specs=[row] * 8 + [pl.BlockSpec(lnx.shape, lambda bi, i: (0, 0))],
        out_specs=row,
        out_shape=jax.ShapeDtypeStruct((b * s, RW_WIDTH), BF16),
        scratch_shapes=[pltpu.VMEM((RW_WIDTH // LANES, LANES, LANES), F32)],
        compiler_params=_params("arbitrary", "arbitrary"),
        name="rwkv_chunks",
    )(*parts, lnx)


def _pad_rows(w, start, total):
    return jnp.zeros((total, w.shape[1]), w.dtype).at[start:start + w.shape[0]].set(w)


def _stack_rows(vectors, rows=8):
    mat = jnp.stack([v.reshape(-1).astype(F32) for v in vectors])
    return jnp.pad(mat, ((0, rows - mat.shape[0]), (0, 0)))


def _even_mixer(h_bf, v_first, b, s, w_in, mu, w_up, w0, a_up, a0, v_up, v0, g_up,
                k_k, k_a, r_k, lnx_g, lnx_b, w_out):
    n = b * s
    n_a = 3 * SB_WIDTH
    n_b = w_in.shape[1] - n_a
    qkv = _matmul(h_bf, w_in[:, :n_a].astype(BF16), BF16)
    w_b = jnp.pad(w_in[:, n_a:], ((0, 0), (0, RW_COLS - n_b))).astype(BF16)
    pb = _matmul(h_bf, w_b, F32, tn=RW_COLS // 4)
    y_a = _stick_breaking(qkv.reshape(b, s, n_a), b, s).reshape(n, SB_WIDTH)

    o_w, o_a, o_g, o_v = 0, RW_DECAY_RANK, RW_DECAY_RANK + RW_ICLR_RANK, RW_DECAY_RANK + RW_ICLR_RANK + RW_GATE_RANK
    wup = _pad_rows(w_up, o_w, RW_LORA).astype(BF16)
    aup = _pad_rows(a_up, o_a, RW_LORA).astype(BF16)
    gup = _pad_rows(g_up, o_g, RW_LORA).astype(BF16)
    if v_up is None:
        vup = jnp.zeros((RW_LORA, RW_WIDTH), BF16)
        v0 = jnp.zeros((RW_WIDTH,), F32)
    else:
        vup = _pad_rows(v_up, o_v, RW_LORA).astype(BF16)
    mu_p = jnp.pad(mu, (0, RW_COLS - n_b)).reshape(1, RW_COLS)
    vecs = _stack_rows([w0, a0, v0, k_k, k_a, r_k])
    parts = _rwkv_prep(pb, b, s, mu_p, wup, aup, gup, vup, vecs, v_first)
    if v_first is None:
        v_first = parts[2]
    y_b = _rwkv_chunks(parts, _stack_rows([lnx_g, lnx_b]), b, s)
    y = jnp.concatenate([y_a, y_b], axis=1)
    return _matmul(y, w_out.astype(BF16), F32), v_first


GDN_QKV = 2 * GDN_KEY_WIDTH + GDN_VAL_WIDTH
GDN_TS = 256
GDN_CHUNK = 64
GDN_HB = 8
GDN_GROUPS = GDN_V_HEADS // GDN_HB
GDN_G_LANE = 64


def _gdn_prep_kernel(p_ref, s_ref, conv_ref, vec_ref, q_ref, k_ref, v_ref, bg_ref, last_ref, *, ts):
    @pl.when(pl.program_id(1) == 0)
    def _():
        last_ref[...] = jnp.zeros_like(last_ref)

    x = p_ref[...]
    last = last_ref[...]
    row8 = lax.broadcasted_iota(jnp.int32, (8, GDN_QKV), 0)
    acc = x * conv_ref[GDN_CONV - 1:GDN_CONV, :]
    for d in range(1, GDN_CONV):
        sh = pltpu.roll(x, d, 0)
        top = jnp.where(row8 < d, pltpu.roll(last, d, 0), sh[:8])
        sh = jnp.concatenate([top, sh[8:]], axis=0)
        acc = acc + sh * conv_ref[GDN_CONV - 1 - d:GDN_CONV - d, :]
    last_ref[...] = x[ts - 8:, :]
    y = _silu(acc)

    kw = GDN_KEY_WIDTH
    for h in range(GDN_QK_HEADS):
        sl = slice(h * GDN_HEAD_DIM, (h + 1) * GDN_HEAD_DIM)
        qh = y[:, sl]
        q_ref[:, sl] = qh * (lax.rsqrt(jnp.sum(qh * qh, axis=-1, keepdims=True) + L2_EPS) * GDN_HEAD_DIM ** -0.5)
        kh = y[:, kw + h * GDN_HEAD_DIM:kw + (h + 1) * GDN_HEAD_DIM]
        k_ref[:, sl] = kh * lax.rsqrt(jnp.sum(kh * kh, axis=-1, keepdims=True) + L2_EPS)
    v_ref[...] = y[:, 2 * kw:]

    small = s_ref[...]
    lane = lax.broadcasted_iota(jnp.int32, small.shape, 1)
    neg_a = vec_ref[0:1, :]
    dt_b = vec_ref[1:2, :]
    vals = jnp.where(lane < GDN_V_HEADS, _sigmoid(small), neg_a * _softplus(small + dt_b))
    src = lax.broadcasted_iota(jnp.int32, (LANES, LANES), 0)
    dst = lax.broadcasted_iota(jnp.int32, (LANES, LANES), 1)
    for grp in range(GDN_GROUPS):
        want = jnp.where(dst < GDN_G_LANE, grp * GDN_HB + dst, GDN_V_HEADS + grp * GDN_HB + dst - GDN_G_LANE)
        sel = jnp.where((src == want) & ((dst % GDN_G_LANE) < GDN_HB), 1.0, 0.0).astype(BF16)
        bg_ref[grp] = _dot_sel_rhs(vals, sel)


def _gdn_prep(p_main, p_small, conv_w, a_log, dt_bias, b, s):
    n = b * s
    ts = min(GDN_TS, s)
    nt = s // ts
    neg_a = jnp.zeros((LANES,), F32).at[GDN_V_HEADS:2 * GDN_V_HEADS].set(-jnp.exp(a_log.astype(F32)))
    dt_b = jnp.zeros((LANES,), F32).at[GDN_V_HEADS:2 * GDN_V_HEADS].set(dt_bias.astype(F32))
    vecs = _stack_rows([neg_a, dt_b])
    conv8 = jnp.pad(conv_w, ((0, 8 - GDN_CONV), (0, 0)))
    row = lambda c: pl.BlockSpec((ts, c), lambda bi, i: (bi * nt + i, 0))
    full = lambda a: pl.BlockSpec(a.shape, lambda bi, i: (0, 0))
    return pl.pallas_call(
        functools.partial(_gdn_prep_kernel, ts=ts),
        grid=(b, nt),
        in_specs=[row(GDN_QKV), row(LANES), full(conv8), full(vecs)],
        out_specs=[row(GDN_KEY_WIDTH), row(GDN_KEY_WIDTH), row(GDN_VAL_WIDTH),
                   pl.BlockSpec((GDN_GROUPS, ts, LANES), lambda bi, i: (0, bi * nt + i, 0))],
        out_shape=[jax.ShapeDtypeStruct((n, GDN_KEY_WIDTH), F32), jax.ShapeDtypeStruct((n, GDN_KEY_WIDTH), F32),
                   jax.ShapeDtypeStruct((n, GDN_VAL_WIDTH), F32), jax.ShapeDtypeStruct((GDN_GROUPS, n, LANES), F32)],
        scratch_shapes=[pltpu.VMEM((8, GDN_QKV), F32)],
        compiler_params=_params("arbitrary", "arbitrary"),
        name="gdn_prep",
    )(p_main, p_small, conv8, vecs)


def _gdn_chunk_kernel(q_ref, k_ref, v_ref, z_ref, bg_ref, ng_ref, o_ref, state_ref, *, c):
    @pl.when(pl.program_id(2) == 0)
    def _():
        state_ref[...] = jnp.zeros_like(state_ref)

    rr = lax.broadcasted_iota(jnp.int32, (c, c), 0)
    cc = lax.broadcasted_iota(jnp.int32, (c, c), 1)
    strict = rr > cc
    incl = rr >= cc
    lower = jnp.where(incl, 1.0, 0.0).astype(BF16)
    upper = jnp.where(rr <= cc, 1.0, 0.0).astype(BF16)
    bg = bg_ref[0]
    cum = _dot_sel_lhs(lower, bg)
    cum_t = sum(lax.dot_general(p, upper, (((0,), (0,)), ((), ())), preferred_element_type=F32)
                for p in _split3(bg))
    eye = jnp.where(lax.broadcasted_iota(jnp.int32, (GDN_HEAD_DIM, GDN_HEAD_DIM), 0)
                    == lax.broadcasted_iota(jnp.int32, (GDN_HEAD_DIM, GDN_HEAD_DIM), 1), 1.0, 0.0)

    for i in range(GDN_HB):
        qk = slice((i // 2) * GDN_HEAD_DIM, (i // 2 + 1) * GDN_HEAD_DIM)
        vs = slice(i * GDN_HEAD_DIM, (i + 1) * GDN_HEAD_DIM)
        q, k, v = q_ref[:, qk], k_ref[:, qk], v_ref[:, vs]
        beta = bg[:, i:i + 1]
        g_col = cum[:, GDN_G_LANE + i:GDN_G_LANE + i + 1]
        g_row = cum_t[GDN_G_LANE + i:GDN_G_LANE + i + 1, :]
        g_last = g_col[c - 1:c, :]
        decay = jnp.where(incl, jnp.exp(jnp.minimum(g_col - g_row, 0.0)), 0.0)
        kb = k * beta
        low = jnp.where(strict, _dot_nt(kb, k, hp=True) * decay, 0.0)
        t_inv = _neumann_inverse(-low, c)
        e_col = jnp.exp(g_col)
        u = _dot(t_inv, v * beta)
        w = _dot(t_inv, kb * e_col)
        attn = _dot_nt(q, k) * decay
        k_dec = k * jnp.exp(g_last - g_col)
        m = jnp.exp(g_last) * eye - _dot_tn(k_dec, w, hp=True)
        nn = _dot_tn(k_dec, u)
        q_hat = q * e_col - _dot(attn, w)
        st = state_ref[i]
        o = _dot(q_hat, st) + _dot(attn, u)
        state_ref[i] = _dot(m, st, hp=True) + nn
        o = o * lax.rsqrt(jnp.mean(o * o, axis=-1, keepdims=True) + GDN_NORM_EPS) * ng_ref[0:1, :]
        o_ref[:, vs] = (o * _silu(z_ref[:, vs])).astype(o_ref.dtype)


def _gdn_chunks(q, k, v, p_main, bg, norm_g, b, s):
    c = min(GDN_CHUNK, s)
    nc = s // c
    kcols = GDN_HB // 2 * GDN_HEAD_DIM
    vcols = GDN_HB * GDN_HEAD_DIM
    z_blk0 = GDN_QKV // vcols
    ng = _stack_rows([norm_g])
    return pl.pallas_call(
        functools.partial(_gdn_chunk_kernel, c=c),
        grid=(b, GDN_GROUPS, nc),
        in_specs=[pl.BlockSpec((c, kcols), lambda bi, g, i: (bi * nc + i, g)),
                  pl.BlockSpec((c, kcols), lambda bi, g, i: (bi * nc + i, g)),
                  pl.BlockSpec((c, vcols), lambda bi, g, i: (bi * nc + i, g)),
                  pl.BlockSpec((c, vcols), lambda bi, g, i: (bi * nc + i, z_blk0 + g)),
                  pl.BlockSpec((1, c, LANES), lambda bi, g, i: (g, bi * nc + i, 0)),
                  pl.BlockSpec(ng.shape, lambda bi, g, i: (0, 0))],
        out_specs=pl.BlockSpec((c, vcols), lambda bi, g, i: (bi * nc + i, g)),
        out_shape=jax.ShapeDtypeStruct((b * s, GDN_VAL_WIDTH), BF16),
        scratch_shapes=[pltpu.VMEM((GDN_HB, GDN_HEAD_DIM, GDN_HEAD_DIM), F32)],
        compiler_params=_params("arbitrary", "arbitrary", "arbitrary"),
        name="gdn_chunks",
    )(q, k, v, p_main, bg, ng)


def _odd_mixer(h_bf, b, s, w_in, conv_w, a_log, dt_bias, norm_g, w_out):
    n_main = GDN_QKV + GDN_VAL_WIDTH
    p_main = _matmul(h_bf, w_in[:, :n_main].astype(BF16), F32)
    w_small = jnp.pad(w_in[:, n_main:], ((0, 0), (0, LANES - 2 * GDN_V_HEADS))).astype(BF16)
    p_small = _matmul(h_bf, w_small, F32)
    q, k, v, bg = _gdn_prep(p_main, p_small, conv_w, a_log, dt_bias, b, s)
    o = _gdn_chunks(q, k, v, p_main, bg, norm_g, b, s)
    return _matmul(o, w_out.astype(BF16), F32, tm=512)


ROUTER_TS = 512
GROUP_SIZE = N_EXPERTS // N_GROUPS


def _beats(other, mine, other_idx, my_idx):
    return jnp.where((other > mine) | ((other == mine) & (other_idx < my_idx)), 1, 0)


def _router_kernel(h_ref, rt_ref, bias_ref, gates_ref, *, ts):
    logits = _dot_nt(rt_ref[...], h_ref[...], hp=True)
    scores = _sigmoid(logits)
    choice = scores + jnp.tile(bias_ref[...], (1, ts // LANES))
    neg_inf = jnp.float32(-jnp.inf)

    grouped = choice.reshape(N_GROUPS, GROUP_SIZE, ts)
    m1 = jnp.max(grouped, axis=1, keepdims=True)
    ties = jnp.sum(jnp.where(grouped == m1, 1, 0), axis=1, keepdims=True)
    m2 = jnp.max(jnp.where(grouped < m1, grouped, neg_inf), axis=1, keepdims=True)
    group_score = (m1 + jnp.where(ties >= 2, m1, m2)).reshape(N_GROUPS, ts)

    g_idx = lax.broadcasted_iota(jnp.int32, (N_GROUPS, ts), 0)
    g_rank = jnp.zeros((N_GROUPS, ts), jnp.int32)
    for g in range(N_GROUPS):
        g_rank = g_rank + _beats(group_score[g:g + 1, :], group_score, g, g_idx)
    g_keep = jnp.where(g_rank < TOPK_GROUPS, 1.0, 0.0).reshape(N_GROUPS, 1, ts)
    e_keep = jnp.broadcast_to(g_keep, (N_GROUPS, GROUP_SIZE, ts)).reshape(N_EXPERTS, ts)
    masked = jnp.where(e_keep > 0.5, choice, neg_inf)

    e_idx = lax.broadcasted_iota(jnp.int32, (N_EXPERTS, ts), 0)
    e_rank = jnp.zeros((N_EXPERTS, ts), jnp.int32)
    for e in range(N_EXPERTS):
        e_rank = e_rank + _beats(masked[e:e + 1, :], masked, e, e_idx)
    top_w = jnp.where(e_rank < TOP_K, scores, 0.0)
    gates_ref[...] = top_w / jnp.sum(top_w, axis=0, keepdims=True) * ROUTED_SCALE


def _router(h, router, router_bias):
    n = h.shape[0]
    ts = min(ROUTER_TS, n)
    rt = router.T.astype(F32)
    bias = jnp.broadcast_to(router_bias.astype(F32)[:, None], (N_EXPERTS, LANES))
    return pl.pallas_call(
        functools.partial(_router_kernel, ts=ts),
        grid=(n // ts,),
        in_specs=[pl.BlockSpec((ts, D_MODEL), lambda i: (i, 0)),
                  pl.BlockSpec(rt.shape, lambda i: (0, 0)),
                  pl.BlockSpec(bias.shape, lambda i: (0, 0))],
        out_specs=pl.BlockSpec((N_EXPERTS, ts), lambda i: (0, i)),
        out_shape=jax.ShapeDtypeStruct((N_EXPERTS, n), F32),
        compiler_params=_params("arbitrary"),
        name="router",
    )(h, rt, bias)


MOE_TS = 512


def _swiglu_hidden(x, w_gu, ff):
    gu = jnp.dot(x, w_gu, preferred_element_type=F32)
    return _silu(gu[:, :ff]) * gu[:, ff:]


def _moe_dense_kernel(hb_ref, h_ref, gates_ref, egu_ref, edn_ref, sgu_ref, sdn_ref, g_ref, b_ref,
                      o_ref, ob_ref, acc_ref):
    e = pl.program_id(1)
    x = hb_ref[...]

    @pl.when(e == 0)
    def _():
        acc_ref[...] = jnp.zeros_like(acc_ref)

    @pl.when(e < N_EXPERTS)
    def _():
        onehot = jnp.where(lax.broadcasted_iota(jnp.int32, (N_EXPERTS, LANES), 0) == e, 1.0, 0.0).astype(BF16)
        gate = _dot_sel_rhs(gates_ref[...], onehot)
        hid = _swiglu_hidden(x, egu_ref[0], EXPERT_FF) * jnp.tile(gate, (1, EXPERT_FF // LANES))
        acc_ref[...] += jnp.dot(hid.astype(BF16), edn_ref[0], preferred_element_type=F32)

    @pl.when(e == N_EXPERTS)
    def _():
        hid = _swiglu_hidden(x, sgu_ref[...], SHARED_FF)
        y = acc_ref[...] + jnp.dot(hid.astype(BF16), sdn_ref[...], preferred_element_type=F32)
        out = _ln_rows(DN_ALPHA * h_ref[...] + y, g_ref[...], b_ref[...])
        o_ref[...] = out
        ob_ref[...] = out.astype(BF16)


def _moe_dense(h, h_bf, gates, e_gu, e_dn, s_gu, s_dn, ln_g, ln_b):
    n, d = h.shape
    ts = min(MOE_TS, n)
    last = N_EXPERTS - 1
    row = lambda c: pl.BlockSpec((ts, c), lambda i, e: (i, 0))
    full = lambda a: pl.BlockSpec(a.shape, lambda i, e: (0,) * a.ndim)
    g2, b2 = ln_g.reshape(1, d), ln_b.reshape(1, d)
    return pl.pallas_call(
        _moe_dense_kernel,
        grid=(n // ts, N_EXPERTS + 1),
        in_specs=[row(d), row(d), row(N_EXPERTS),
                  pl.BlockSpec((1,) + e_gu.shape[1:], lambda i, e: (jnp.minimum(e, last), 0, 0)),
                  pl.BlockSpec((1,) + e_dn.shape[1:], lambda i, e: (jnp.minimum(e, last), 0, 0)),
                  full(s_gu), full(s_dn), full(g2), full(b2)],
        out_specs=[row(d), row(d)],
        out_shape=[jax.ShapeDtypeStruct((n, d), F32), jax.ShapeDtypeStruct((n, d), BF16)],
        scratch_shapes=[pltpu.VMEM((ts, d), F32)],
        compiler_params=_params("arbitrary", "arbitrary"),
        name="moe_dense",
    )(h_bf, h, gates, e_gu, e_dn, s_gu, s_dn, g2, b2)


def _ffn(h, h_bf, router, router_bias, e_gate, e_up, e_down, s_gate, s_up, s_down, ln_g, ln_b):
    gates = _router(h, router, router_bias).T
    e_gu = jnp.concatenate([e_gate, e_up], axis=2).astype(BF16)
    s_gu = jnp.concatenate([s_gate, s_up], axis=1).astype(BF16)
    return _moe_dense(h, h_bf, gates, e_gu, e_down.astype(BF16), s_gu, s_down.astype(BF16), ln_g, ln_b)


def _trunk(x, layers):
    b, s, d = x.shape
    n = b * s
    x = x.reshape(n, d)
    x_bf = x.astype(BF16)
    v_first = None
    for i, (mixer, ffn) in enumerate(layers):
        if i % 2 == 0:
            mix, v_first = _even_mixer(x_bf, v_first, b, s, *mixer)
        else:
            mix = _odd_mixer(x_bf, b, s, *mixer)
        h, h_bf = _add_ln(x, mix, ffn[0], ffn[1])
        x, x_bf = _ffn(h, h_bf, *ffn[2:])
    return x.reshape(b, s, d)


def kernel(x, l0_w_in, l0_mu, l0_w_up, l0_w0, l0_a_up, l0_a0, l0_g_up, l0_k_k, l0_k_a, l0_r_k, l0_lnx_g, l0_lnx_b, l0_w_out, l0_ln1_g, l0_ln1_b, l0_router, l0_router_bias, l0_e_gate, l0_e_up, l0_e_down, l0_s_gate, l0_s_up, l0_s_down, l0_ln2_g, l0_ln2_b, l1_w_in, l1_conv, l1_a_log, l1_dt_bias, l1_norm_g, l1_w_out, l1_ln1_g, l1_ln1_b, l1_router, l1_router_bias, l1_e_gate, l1_e_up, l1_e_down, l1_s_gate, l1_s_up, l1_s_down, l1_ln2_g, l1_ln2_b, l2_w_in, l2_mu, l2_w_up, l2_w0, l2_a_up, l2_a0, l2_v_up, l2_v0, l2_g_up, l2_k_k, l2_k_a, l2_r_k, l2_lnx_g, l2_lnx_b, l2_w_out, l2_ln1_g, l2_ln1_b, l2_router, l2_router_bias, l2_e_gate, l2_e_up, l2_e_down, l2_s_gate, l2_s_up, l2_s_down, l2_ln2_g, l2_ln2_b, l3_w_in, l3_conv, l3_a_log, l3_dt_bias, l3_norm_g, l3_w_out, l3_ln1_g, l3_ln1_b, l3_router, l3_router_bias, l3_e_gate, l3_e_up, l3_e_down, l3_s_gate, l3_s_up, l3_s_down, l3_ln2_g, l3_ln2_b):
    layers = [
        ((l0_w_in, l0_mu, l0_w_up, l0_w0, l0_a_up, l0_a0, None, None, l0_g_up, l0_k_k, l0_k_a, l0_r_k,
          l0_lnx_g, l0_lnx_b, l0_w_out),
         (l0_ln1_g, l0_ln1_b, l0_router, l0_router_bias, l0_e_gate, l0_e_up, l0_e_down,
          l0_s_gate, l0_s_up, l0_s_down, l0_ln2_g, l0_ln2_b)),
        ((l1_w_in, l1_conv, l1_a_log, l1_dt_bias, l1_norm_g, l1_w_out),
         (l1_ln1_g, l1_ln1_b, l1_router, l1_router_bias, l1_e_gate, l1_e_up, l1_e_down,
          l1_s_gate, l1_s_up, l1_s_down, l1_ln2_g, l1_ln2_b)),
        ((l2_w_in, l2_mu, l2_w_up, l2_w0, l2_a_up, l2_a0, l2_v_up, l2_v0, l2_g_up, l2_k_k, l2_k_a, l2_r_k,
          l2_lnx_g, l2_lnx_b, l2_w_out),
         (l2_ln1_g, l2_ln1_b, l2_router, l2_router_bias, l2_e_gate, l2_e_up, l2_e_down,
          l2_s_gate, l2_s_up, l2_s_down, l2_ln2_g, l2_ln2_b)),
        ((l3_w_in, l3_conv, l3_a_log, l3_dt_bias, l3_norm_g, l3_w_out),
         (l3_ln1_g, l3_ln1_b, l3_router, l3_router_bias, l3_e_gate, l3_e_up, l3_e_down,
          l3_s_gate, l3_s_up, l3_s_down, l3_ln2_g, l3_ln2_b)),
    ]
    return _trunk(x, layers)
```

```python
import functools
import math

import jax
import jax.numpy as jnp
from jax import lax
from jax.experimental import pallas as pl
from jax.experimental.pallas import tpu as pltpu

D_MODEL = 2048
DEPTH = 4
SB_HEADS = 16
SB_HEAD_DIM = 64
SB_WIDTH = SB_HEADS * SB_HEAD_DIM
RW_HEADS = 16
RW_HEAD_DIM = 64
RW_WIDTH = RW_HEADS * RW_HEAD_DIM
RW_DECAY_RANK = 96
RW_ICLR_RANK = 96
RW_VRES_RANK = 64
RW_GATE_RANK = 256
RW_GN_EPS = 64e-5
GDN_QK_HEADS = 16
GDN_V_HEADS = 32
GDN_HEAD_DIM = 128
GDN_KEY_WIDTH = GDN_QK_HEADS * GDN_HEAD_DIM
GDN_VAL_WIDTH = GDN_V_HEADS * GDN_HEAD_DIM
GDN_CONV = 4
GDN_NORM_EPS = 1e-6
N_EXPERTS = 64
TOP_K = 8
N_GROUPS = 8
TOPK_GROUPS = 4
EXPERT_FF = 384
SHARED_FF = 384
ROUTED_SCALE = 2.5
DN_ALPHA = (2 * DEPTH) ** 0.25
LN_EPS = 1e-5
L2_EPS = 1e-6

LANES = 128
VMEM_LIMIT = 56 * 1024 * 1024
F32 = jnp.float32
BF16 = jnp.bfloat16
HI = lax.Precision.HIGHEST


def _params(*sem):
    return pltpu.CompilerParams(dimension_semantics=sem, vmem_limit_bytes=VMEM_LIMIT)


def _contract(a, b, dims):
    return lax.dot_general(a.astype(BF16), b.astype(BF16), (dims, ((), ())), preferred_element_type=F32)


def _dot(a, b):
    return _contract(a, b, ((1,), (0,)))


def _dot_nt(a, b):
    return _contract(a, b, ((1,), (1,)))


def _dot_tn(a, b):
    return _contract(a, b, ((0,), (0,)))


def _sigmoid(x):
    return 1.0 / (1.0 + jnp.exp(-x))


def _softplus(x):
    return jnp.maximum(x, 0.0) + jnp.log(1.0 + jnp.exp(-jnp.abs(x)))


def _silu(x):
    return x * _sigmoid(x)


def _matmul_kernel(x_ref, w_ref, o_ref):
    o_ref[...] = jnp.dot(x_ref[...], w_ref[...], preferred_element_type=F32).astype(o_ref.dtype)


def _matmul(x, w, out_dtype, tm=1024, tn=1024):
    n, k = x.shape
    m = w.shape[1]
    tm = min(tm, n)
    tn = min(tn, m)
    assert n % tm == 0 and m % tn == 0, (n, m, tm, tn)
    return pl.pallas_call(
        _matmul_kernel,
        grid=(m // tn, n // tm),
        in_specs=[pl.BlockSpec((tm, k), lambda j, i: (i, 0)),
                  pl.BlockSpec((k, tn), lambda j, i: (0, j))],
        out_specs=pl.BlockSpec((tm, tn), lambda j, i: (i, j)),
        out_shape=jax.ShapeDtypeStruct((n, m), out_dtype),
        compiler_params=_params("arbitrary", "arbitrary"),
        name="matmul",
    )(x, w)


def _ln_rows(y, g, b):
    mean = jnp.mean(y, axis=-1, keepdims=True)
    yc = y - mean
    var = jnp.mean(yc * yc, axis=-1, keepdims=True)
    return yc * lax.rsqrt(var + LN_EPS) * g + b


PACK_ROWS = D_MODEL // 2 // LANES


def _bf16_bits(x):
    return pltpu.bitcast(x.astype(BF16).astype(F32), jnp.uint32)


def _store_packed(ref, x, rows):
    half = D_MODEL // 2
    words = (_bf16_bits(x[:, half:]) & jnp.uint32(0xFFFF0000)) | (_bf16_bits(x[:, :half]) >> 16)
    for s in range(PACK_ROWS):
        ref[pl.ds(s, rows, stride=PACK_ROWS), :] = words[:, s * LANES:(s + 1) * LANES]


def _load_packed(ref, start, rows):
    lo, hi = [], []
    for s in range(PACK_ROWS):
        words = ref[pl.ds(start + s, rows, stride=PACK_ROWS), :]
        lo.append(pltpu.bitcast(words << 16, F32).astype(BF16))
        hi.append(pltpu.bitcast(words & jnp.uint32(0xFFFF0000), F32).astype(BF16))
    return jnp.concatenate(lo + hi, axis=1)


def _add_ln_kernel(x_ref, m_ref, g_ref, b_ref, o_ref, ob_ref, op_ref, *, tm):
    y = DN_ALPHA * x_ref[...] + m_ref[...].astype(F32)
    out = _ln_rows(y, g_ref[...], b_ref[...])
    o_ref[...] = out
    ob_ref[...] = out.astype(BF16)
    _store_packed(op_ref, out, tm)


def _add_ln(x, mix, g, b, tm=512):
    n, d = x.shape
    tm = min(tm, n)
    row = pl.BlockSpec((tm, d), lambda i: (i, 0))
    vec = pl.BlockSpec((1, d), lambda i: (0, 0))
    return pl.pallas_call(
        functools.partial(_add_ln_kernel, tm=tm),
        grid=(n // tm,),
        in_specs=[row, row, vec, vec],
        out_specs=[row, row, pl.BlockSpec((tm * PACK_ROWS, LANES), lambda i: (i, 0))],
        out_shape=[jax.ShapeDtypeStruct((n, d), F32), jax.ShapeDtypeStruct((n, d), BF16),
                   jax.ShapeDtypeStruct((n * PACK_ROWS, LANES), jnp.uint32)],
        compiler_params=_params("arbitrary"),
        name="add_ln",
    )(x, mix, g.reshape(1, d), b.reshape(1, d))


SB_TQ = 256
SB_TK = 128
SB_UNDERFLOW = -104.0


def _sb_kernel(q_ref, k_ref, v_ref, o_ref, *, tq, tk):
    i = pl.program_id(2)
    q = q_ref[0] * jnp.asarray(SB_HEAD_DIM ** -0.5, BF16)
    lane = lax.broadcasted_iota(jnp.int32, (tq, LANES), 1)
    zero = jnp.zeros_like(q)
    q_heads = (jnp.where(lane < SB_HEAD_DIM, q, zero), jnp.where(lane >= SB_HEAD_DIM, q, zero))
    q_pos = i * tq + lax.broadcasted_iota(jnp.int32, (tq, tk), 0)
    k_off = lax.broadcasted_iota(jnp.int32, (tq, tk), 1)
    r2 = lax.broadcasted_iota(jnp.int32, (2 * tk, tk + LANES), 0) % tk
    c2 = lax.broadcasted_iota(jnp.int32, (2 * tk, tk + LANES), 1)
    cum = jnp.where((r2 > c2) | (c2 >= tk), 1.0, 0.0).astype(BF16)
    n_diag = tq // tk
    first_diag = i * n_diag

    def block(j, carry, on_diagonal):
        start = pl.multiple_of(j * tk, tk)
        ks = k_ref[0, pl.ds(start, tk), :]
        vs = v_ref[0, pl.ds(start, tk), :]
        causal = (start + k_off) < q_pos
        new = []
        for h in range(2):
            acc, later_blocks = carry[2 * h], carry[2 * h + 1]
            z = _dot_nt(q_heads[h], ks)
            sp = _softplus(z)
            log_keep = jnp.where(causal, -sp, 0.0) if on_diagonal else -sp
            hi = log_keep.astype(BF16)
            lo = (log_keep - hi.astype(F32)).astype(BF16)
            sums = jnp.dot(jnp.concatenate([hi, lo], axis=1), cum, preferred_element_type=F32)
            later = sums[:, :tk] + jnp.tile(later_blocks, (1, tk // LANES))
            w = jnp.exp(z - sp + later)
            if on_diagonal:
                w = jnp.where(causal, w, 0.0)
            acc = acc + jnp.dot(w.astype(BF16), vs, preferred_element_type=F32)
            new += [acc, later_blocks + sums[:, tk:]]
        return tuple(new)

    carry = tuple(jnp.zeros((tq, LANES), F32) for _ in range(4))
    for d in range(n_diag):
        carry = block(first_diag + n_diag - 1 - d, carry, True)

    def live(carry):
        return jnp.max(jnp.maximum(carry[1], carry[3])) > SB_UNDERFLOW

    def cond(state):
        return (state[0] >= 0) & state[1]

    def body(state):
        carry = block(state[0], state[2], False)
        return state[0] - 1, live(carry), carry

    _, _, res = lax.while_loop(cond, body, (first_diag - 1, live(carry), carry))
    o_ref[0] = jnp.where(lane < SB_HEAD_DIM, res[0], res[2]).astype(o_ref.dtype)


def _stick_breaking(qkv, b, s):
    tq = min(SB_TQ, s)
    tk = min(SB_TK, s)
    pairs = SB_WIDTH // LANES
    return pl.pallas_call(
        functools.partial(_sb_kernel, tq=tq, tk=tk),
        grid=(b, pairs, s // tq),
        in_specs=[pl.BlockSpec((1, tq, LANES), lambda bi, p, i: (bi, i, p)),
                  pl.BlockSpec((1, s, LANES), lambda bi, p, i: (bi, 0, pairs + p)),
                  pl.BlockSpec((1, s, LANES), lambda bi, p, i: (bi, 0, 2 * pairs + p))],
        out_specs=pl.BlockSpec((1, tq, LANES), lambda bi, p, i: (bi, i, p)),
        out_shape=jax.ShapeDtypeStruct((b, s, SB_WIDTH), BF16),
        compiler_params=_params("arbitrary", "arbitrary", "arbitrary"),
        name="stick_breaking",
    )(qkv, qkv, qkv)


def _split3(x):
    hi = x.astype(BF16)
    r1 = x - hi.astype(F32)
    mid = r1.astype(BF16)
    lo = (r1 - mid.astype(F32)).astype(BF16)
    return hi, mid, lo


def _dot_sel_rhs(x, sel):
    return sum(jnp.dot(p, sel, preferred_element_type=F32) for p in _split3(x))


def _dot_sel_lhs(sel, x):
    return sum(jnp.dot(sel, p, preferred_element_type=F32) for p in _split3(x))


def _head_sum_matrices(width, head_dim):
    heads = width // head_dim
    assert heads <= LANES
    c = lax.broadcasted_iota(jnp.int32, (width, LANES), 0) // head_dim
    h = lax.broadcasted_iota(jnp.int32, (width, LANES), 1)
    gather = jnp.where(c == h, 1.0, 0.0).astype(BF16)
    ct = lax.broadcasted_iota(jnp.int32, (LANES, width), 1) // head_dim
    ht = lax.broadcasted_iota(jnp.int32, (LANES, width), 0)
    spread = jnp.where(ct == ht, 1.0, 0.0).astype(BF16)
    return gather, spread


def _per_head_sum(x, gather, spread):
    return _dot_sel_rhs(_dot_sel_rhs(x, gather), spread)


def _neumann_inverse(a, size, nilpotent=None):
    nilpotent = size if nilpotent is None else nilpotent
    eye = jnp.where(lax.broadcasted_iota(jnp.int32, (size, size), 0)
                    == lax.broadcasted_iota(jnp.int32, (size, size), 1), 1.0, 0.0)
    inv = eye + a
    power = a
    span = 2
    while span < nilpotent:
        power = _dot(power, power)
        inv = inv + _dot(inv, power)
        span *= 2
    return inv


RW_LORA = 512
RW_COLS = 3 * RW_WIDTH + RW_LORA
RW_TS = 256
RW_CHUNK = 64
RW_STACK = 2


def _rwkv_prep_kernel(pb_ref, mu_ref, wup_ref, aup_ref, gup_ref, vup_ref, vec_ref, vfirst_ref,
                      r_ref, k_ref, v_ref, kk_ref, a_ref, lw_ref, g_ref, bonus_ref,
                      last_ref, *, ts, has_vres):
    @pl.when(pl.program_id(1) == 0)
    def _():
        last_ref[...] = jnp.zeros_like(last_ref)

    x = pb_ref[...]
    prev = pltpu.roll(x, 1, 0)
    first_row = lax.broadcasted_iota(jnp.int32, x.shape, 0) == 0
    prev = jnp.where(first_row, jnp.broadcast_to(last_ref[0:1, :], x.shape), prev)
    last_ref[0:1, :] = x[ts - 1:ts, :]
    x = x + (prev - x) * mu_ref[...]

    w = RW_WIDTH
    r, k, v, lora = x[:, :w], x[:, w:2 * w], x[:, 2 * w:3 * w], x[:, 3 * w:]
    w0, a0, v0, k_k, k_a, r_k = (vec_ref[i:i + 1, :] for i in range(6))
    w_log = -_softplus(-(w0 + _dot(jnp.tanh(lora), wup_ref[...]))) - 0.5
    lw_ref[...] = -jnp.exp(w_log)
    if has_vres:
        v = v + (vfirst_ref[...] - v) * _sigmoid(v0 + _dot(lora, vup_ref[...]))
    a = _sigmoid(a0 + _dot(lora, aup_ref[...]))
    g_ref[...] = _dot(_sigmoid(lora), gup_ref[...])
    gather, spread = _head_sum_matrices(w, RW_HEAD_DIM)
    kk = k * k_k
    kk = kk * lax.rsqrt(_per_head_sum(kk * kk, gather, spread) + L2_EPS)
    k = k * (1.0 + (a - 1.0) * k_a)
    bonus_ref[...] = _per_head_sum(r * k * r_k, gather, spread) * v
    r_ref[...] = r
    k_ref[...] = k
    v_ref[...] = v
    kk_ref[...] = kk
    a_ref[...] = a


def _rwkv_prep(pb, b, s, mu, wup, aup, gup, vup, vecs, v_first):
    n = b * s
    ts = min(RW_TS, s)
    nt = s // ts
    has_vres = v_first is not None
    if not has_vres:
        v_first = jnp.zeros((8, RW_WIDTH), F32)
        vf_spec = pl.BlockSpec((8, RW_WIDTH), lambda bi, i: (0, 0))
    else:
        vf_spec = pl.BlockSpec((ts, RW_WIDTH), lambda bi, i: (bi * nt + i, 0))
    row = lambda c: pl.BlockSpec((ts, c), lambda bi, i: (bi * nt + i, 0))
    full = lambda a: pl.BlockSpec(a.shape, lambda bi, i: (0, 0))
    outs = [jax.ShapeDtypeStruct((n, RW_WIDTH), F32)] * 8
    return pl.pallas_call(
        functools.partial(_rwkv_prep_kernel, ts=ts, has_vres=has_vres),
        grid=(b, nt),
        in_specs=[row(RW_COLS), full(mu), full(wup), full(aup), full(gup), full(vup), full(vecs), vf_spec],
        out_specs=[row(RW_WIDTH)] * 8,
        out_shape=outs,
        scratch_shapes=[pltpu.VMEM((8, RW_COLS), F32)],
        compiler_params=_params("arbitrary", "arbitrary"),
        name="rwkv_prep",
    )(pb, mu, wup, aup, gup, vup, vecs, v_first)


def _rwkv_chunk_kernel(r_ref, k_ref, v_ref, kk_ref, a_ref, lw_ref, g_ref, bonus_ref, lnx_ref,
                       o_ref, state_ref, *, c):
    @pl.when(pl.program_id(1) == 0)
    def _():
        state_ref[...] = jnp.zeros_like(state_ref)

    tri = jnp.where(lax.broadcasted_iota(jnp.int32, (c, c), 0)
                    >= lax.broadcasted_iota(jnp.int32, (c, c), 1), 1.0, 0.0).astype(BF16)

    lw = lw_ref[...]
    cum = _dot_sel_lhs(tri, lw)
    total = cum[c - 1:c, :]
    kk = kk_ref[...]
    b_vec = kk * a_ref[...]
    dec_out = jnp.exp(total - cum)
    inv_cum = jnp.exp(-cum)
    r_in = r_ref[...] * jnp.exp(cum)
    a_in = -kk * jnp.exp(cum - lw)
    b_out = b_vec * inv_cum
    k_out = k_ref[...] * inv_cum
    b_end = b_vec * dec_out
    k_end = k_ref[...] * dec_out
    v_all = v_ref[...]
    ones = jnp.ones((c, LANES), BF16)
    p_end = jnp.exp(sum(lax.dot_general(piece, ones, (((0,), (0,)), ((), ())), preferred_element_type=F32)
                        for piece in _split3(lw)))

    lane = lax.broadcasted_iota(jnp.int32, (c, LANES), 1)
    first = lane < RW_HEAD_DIM
    blk_r = lax.broadcasted_iota(jnp.int32, (LANES, LANES), 0)
    blk_c = lax.broadcasted_iota(jnp.int32, (LANES, LANES), 1)
    same_head = (blk_r // RW_HEAD_DIM) == (blk_c // RW_HEAD_DIM)

    rows = 2 * RW_STACK * c
    srr = lax.broadcasted_iota(jnp.int32, (rows, rows), 0)
    scc = lax.broadcasted_iota(jnp.int32, (rows, rows), 1)
    same_blk = (srr // c) == (scc // c)
    strict = same_blk & (srr > scc)
    incl = same_blk & (srr >= scc)
    stack = lambda parts: jnp.concatenate(parts, axis=0)
    twice = lambda xs: stack([x for x in xs for _ in range(2)])
    by_head = lambda xs: stack([jnp.where(keep, x, 0.0) for x in xs for keep in (first, ~first)])
    pick = lambda x, j: jnp.where(first, x[2 * j * c:(2 * j + 1) * c], x[(2 * j + 1) * c:(2 * j + 2) * c])

    ys = []
    for grp in range(RW_WIDTH // LANES // RW_STACK):
        sls = [slice(p * LANES, (p + 1) * LANES) for p in range(grp * RW_STACK, (grp + 1) * RW_STACK)]
        ai, ri = [a_in[:, sl] for sl in sls], [r_in[:, sl] for sl in sls]
        bo, ko, v = twice([b_out[:, sl] for sl in sls]), twice([k_out[:, sl] for sl in sls]), twice([v_all[:, sl] for sl in sls])
        ai_h, ri_h = by_head(ai), by_head(ri)
        a_ab = jnp.where(strict, _dot_nt(ai_h, bo), 0.0)
        a_ak = jnp.where(strict, _dot_nt(ai_h, ko), 0.0)
        a_rb = jnp.where(incl, _dot_nt(ri_h, bo), 0.0)
        a_rk = jnp.where(incl, _dot_nt(ri_h, ko), 0.0)
        t_inv = _neumann_inverse(a_ab, rows, nilpotent=c)
        w_rows = _dot(t_inv, twice(ai))
        u_rows = _dot(t_inv, _dot(a_ak, v))
        w_t = [pick(w_rows, j) for j in range(RW_STACK)]
        u0 = [pick(u_rows, j) for j in range(RW_STACK)]
        q_rows = _dot(a_rb, twice(w_t))
        y_rows = _dot(a_rb, twice(u0)) + _dot(a_rk, v)
        for j, sl in enumerate(sls):
            p = grp * RW_STACK + j
            be, ke, vp = b_end[:, sl], k_end[:, sl], v_all[:, sl]
            m_t = jnp.where(same_head, _dot_tn(be, w_t[j]), 0.0)
            n_t = jnp.where(same_head, _dot_tn(be, u0[j]) + _dot_tn(ke, vp), 0.0)
            st = state_ref[p]
            ys.append(_dot(ri[j] + pick(q_rows, j), st) + pick(y_rows, j))
            state_ref[p] = p_end[sl, :] * st + (_dot(m_t, st) + n_t)
    y = jnp.concatenate(ys, axis=1)

    gather, spread = _head_sum_matrices(RW_WIDTH, RW_HEAD_DIM)
    mean = _per_head_sum(y, gather, spread) * (1.0 / RW_HEAD_DIM)
    yc = y - mean
    var = _per_head_sum(yc * yc, gather, spread) * (1.0 / RW_HEAD_DIM)
    y = yc * lax.rsqrt(var + RW_GN_EPS) * lnx_ref[0:1, :] + lnx_ref[1:2, :]
    o_ref[...] = ((y + bonus_ref[...]) * g_ref[...]).astype(o_ref.dtype)


def _rwkv_chunks(parts, lnx, b, s):
    c = min(RW_CHUNK, s)
    nc = s // c
    row = pl.BlockSpec((c, RW_WIDTH), lambda bi, i: (bi * nc + i, 0))
    return pl.pallas_call(
        functools.partial(_rwkv_chunk_kernel, c=c),
        grid=(b, nc),
        in_specs=[row] * 8 + [pl.BlockSpec(lnx.shape, lambda bi, i: (0, 0))],
        out_specs=row,
        out_shape=jax.ShapeDtypeStruct((b * s, RW_WIDTH), BF16),
        scratch_shapes=[pltpu.VMEM((RW_WIDTH // LANES, LANES, LANES), F32)],
        compiler_params=_params("arbitrary", "arbitrary"),
        name="rwkv_chunks",
    )(*parts, lnx)


def _pad_rows(w, start, total):
    return jnp.zeros((total, w.shape[1]), w.dtype).at[start:start + w.shape[0]].set(w)


def _stack_rows(vectors, rows=8):
    mat = jnp.stack([v.reshape(-1).astype(F32) for v in vectors])
    return jnp.pad(mat, ((0, rows - mat.shape[0]), (0, 0)))


def _even_mixer(h_bf, v_first, b, s, w_in, mu, w_up, w0, a_up, a0, v_up, v0, g_up,
                k_k, k_a, r_k, lnx_g, lnx_b, w_out):
    n = b * s
    n_a = 3 * SB_WIDTH
    n_b = w_in.shape[1] - n_a
    qkv = _matmul(h_bf, w_in[:, :n_a].astype(BF16), BF16)
    w_b = jnp.pad(w_in[:, n_a:], ((0, 0), (0, RW_COLS - n_b))).astype(BF16)
    pb = _matmul(h_bf, w_b, F32, tn=RW_COLS // 4)
    y_a = _stick_breaking(qkv.reshape(b, s, n_a), b, s).reshape(n, SB_WIDTH)

    o_w, o_a, o_g, o_v = 0, RW_DECAY_RANK, RW_DECAY_RANK + RW_ICLR_RANK, RW_DECAY_RANK + RW_ICLR_RANK + RW_GATE_RANK
    wup = _pad_rows(w_up, o_w, RW_LORA).astype(BF16)
    aup = _pad_rows(a_up, o_a, RW_LORA).astype(BF16)
    gup = _pad_rows(g_up, o_g, RW_LORA).astype(BF16)
    if v_up is None:
        vup = jnp.zeros((RW_LORA, RW_WIDTH), BF16)
        v0 = jnp.zeros((RW_WIDTH,), F32)
    else:
        vup = _pad_rows(v_up, o_v, RW_LORA).astype(BF16)
    mu_p = jnp.pad(mu, (0, RW_COLS - n_b)).reshape(1, RW_COLS)
    vecs = _stack_rows([w0, a0, v0, k_k, k_a, r_k])
    parts = _rwkv_prep(pb, b, s, mu_p, wup, aup, gup, vup, vecs, v_first)
    if v_first is None:
        v_first = parts[2]
    y_b = _rwkv_chunks(parts, _stack_rows([lnx_g, lnx_b]), b, s)
    y = jnp.concatenate([y_a, y_b], axis=1)
    return _matmul(y, w_out.astype(BF16), F32), v_first


GDN_QKV = 2 * GDN_KEY_WIDTH + GDN_VAL_WIDTH
GDN_TS = 256
GDN_CHUNK = 64
GDN_HB = 8
GDN_STACK = 4
GDN_GROUPS = GDN_V_HEADS // GDN_HB
GDN_G_LANE = 64


def _gdn_prep_kernel(p_ref, s_ref, conv_ref, vec_ref, q_ref, k_ref, v_ref, bg_ref, last_ref, *, ts):
    @pl.when(pl.program_id(1) == 0)
    def _():
        last_ref[...] = jnp.zeros_like(last_ref)

    x = p_ref[...]
    last = last_ref[...]
    row8 = lax.broadcasted_iota(jnp.int32, (8, GDN_QKV), 0)
    acc = x * conv_ref[GDN_CONV - 1:GDN_CONV, :]
    for d in range(1, GDN_CONV):
        sh = pltpu.roll(x, d, 0)
        top = jnp.where(row8 < d, pltpu.roll(last, d, 0), sh[:8])
        sh = jnp.concatenate([top, sh[8:]], axis=0)
        acc = acc + sh * conv_ref[GDN_CONV - 1 - d:GDN_CONV - d, :]
    last_ref[...] = x[ts - 8:, :]
    y = _silu(acc)

    kw = GDN_KEY_WIDTH
    for h in range(GDN_QK_HEADS):
        sl = slice(h * GDN_HEAD_DIM, (h + 1) * GDN_HEAD_DIM)
        qh = y[:, sl]
        q_ref[:, sl] = qh * (lax.rsqrt(jnp.sum(qh * qh, axis=-1, keepdims=True) + L2_EPS) * GDN_HEAD_DIM ** -0.5)
        kh = y[:, kw + h * GDN_HEAD_DIM:kw + (h + 1) * GDN_HEAD_DIM]
        k_ref[:, sl] = kh * lax.rsqrt(jnp.sum(kh * kh, axis=-1, keepdims=True) + L2_EPS)
    v_ref[...] = y[:, 2 * kw:]

    small = s_ref[...]
    lane = lax.broadcasted_iota(jnp.int32, small.shape, 1)
    neg_a = vec_ref[0:1, :]
    dt_b = vec_ref[1:2, :]
    vals = jnp.where(lane < GDN_V_HEADS, _sigmoid(small), neg_a * _softplus(small + dt_b))
    src = lax.broadcasted_iota(jnp.int32, (LANES, LANES), 0)
    dst = lax.broadcasted_iota(jnp.int32, (LANES, LANES), 1)
    for grp in range(GDN_GROUPS):
        want = jnp.where(dst < GDN_G_LANE, grp * GDN_HB + dst, GDN_V_HEADS + grp * GDN_HB + dst - GDN_G_LANE)
        sel = jnp.where((src == want) & ((dst % GDN_G_LANE) < GDN_HB), 1.0, 0.0).astype(BF16)
        bg_ref[grp] = _dot_sel_rhs(vals, sel)


def _gdn_prep(p_main, p_small, conv_w, a_log, dt_bias, b, s):
    n = b * s
    ts = min(GDN_TS, s)
    nt = s // ts
    neg_a = jnp.zeros((LANES,), F32).at[GDN_V_HEADS:2 * GDN_V_HEADS].set(-jnp.exp(a_log.astype(F32)))
    dt_b = jnp.zeros((LANES,), F32).at[GDN_V_HEADS:2 * GDN_V_HEADS].set(dt_bias.astype(F32))
    vecs = _stack_rows([neg_a, dt_b])
    conv8 = jnp.pad(conv_w, ((0, 8 - GDN_CONV), (0, 0)))
    row = lambda c: pl.BlockSpec((ts, c), lambda bi, i: (bi * nt + i, 0))
    full = lambda a: pl.BlockSpec(a.shape, lambda bi, i: (0, 0))
    return pl.pallas_call(
        functools.partial(_gdn_prep_kernel, ts=ts),
        grid=(b, nt),
        in_specs=[row(GDN_QKV), row(LANES), full(conv8), full(vecs)],
        out_specs=[row(GDN_KEY_WIDTH), row(GDN_KEY_WIDTH), row(GDN_VAL_WIDTH),
                   pl.BlockSpec((GDN_GROUPS, ts, LANES), lambda bi, i: (0, bi * nt + i, 0))],
        out_shape=[jax.ShapeDtypeStruct((n, GDN_KEY_WIDTH), F32), jax.ShapeDtypeStruct((n, GDN_KEY_WIDTH), F32),
                   jax.ShapeDtypeStruct((n, GDN_VAL_WIDTH), F32), jax.ShapeDtypeStruct((GDN_GROUPS, n, LANES), F32)],
        scratch_shapes=[pltpu.VMEM((8, GDN_QKV), F32)],
        compiler_params=_params("arbitrary", "arbitrary"),
        name="gdn_prep",
    )(p_main, p_small, conv8, vecs)


def _gdn_chunk_kernel(q_ref, k_ref, v_ref, z_ref, bg_ref, ng_ref, o_ref, state_ref, *, c):
    @pl.when(pl.program_id(2) == 0)
    def _():
        state_ref[...] = jnp.zeros_like(state_ref)

    lower = jnp.where(lax.broadcasted_iota(jnp.int32, (c, c), 0)
                      >= lax.broadcasted_iota(jnp.int32, (c, c), 1), 1.0, 0.0).astype(BF16)
    bg = bg_ref[0]
    cum = _dot_sel_lhs(lower, bg)

    rows = GDN_STACK * c
    rr = lax.broadcasted_iota(jnp.int32, (rows, rows), 0)
    cc = lax.broadcasted_iota(jnp.int32, (rows, rows), 1)
    same = (rr // c) == (cc // c)
    strict = same & (rr > cc)
    incl = same & (rr >= cc)
    diag = rr == cc
    all_ones = jnp.ones((rows, rows), BF16)
    stack = lambda parts: jnp.concatenate(parts, axis=0)

    for grp in range(GDN_HB // GDN_STACK):
        heads = range(grp * GDN_STACK, (grp + 1) * GDN_STACK)
        qk = lambda i: slice((i // 2) * GDN_HEAD_DIM, (i // 2 + 1) * GDN_HEAD_DIM)
        vs = lambda i: slice(i * GDN_HEAD_DIM, (i + 1) * GDN_HEAD_DIM)
        q = stack([q_ref[:, qk(i)] for i in heads])
        k = stack([k_ref[:, qk(i)] for i in heads])
        v = stack([v_ref[:, vs(i)] for i in heads])
        beta = stack([bg[:, i:i + 1] for i in heads])
        g_col = stack([cum[:, GDN_G_LANE + i:GDN_G_LANE + i + 1] for i in heads])
        g_end = stack([jnp.broadcast_to(cum[c - 1:c, GDN_G_LANE + i:GDN_G_LANE + i + 1], (c, 1)) for i in heads])
        g_row = _dot_sel_lhs(all_ones, jnp.where(diag, g_col, 0.0))
        decay = jnp.where(incl, jnp.exp(jnp.minimum(g_col - g_row, 0.0)), 0.0)
        kb = k * beta
        low = jnp.where(strict, _dot_nt(kb, k) * decay, 0.0)
        t_inv = _neumann_inverse(-low, rows, nilpotent=c)
        e_col = jnp.exp(g_col)
        u = _dot(t_inv, v * beta)
        w = _dot(t_inv, kb * e_col)
        attn = _dot_nt(q, k) * decay
        k_dec = k * jnp.exp(g_end - g_col)
        q_hat = q * e_col - _dot(attn, w)
        o_in = _dot(attn, u)
        for j, i in enumerate(heads):
            rs = slice(j * c, (j + 1) * c)
            ktw = _dot_tn(k_dec[rs], w[rs])
            nn = _dot_tn(k_dec[rs], u[rs])
            st = state_ref[i]
            o = _dot(q_hat[rs], st) + o_in[rs]
            state_ref[i] = jnp.exp(g_end[rs][0:1, :]) * st + (nn - _dot(ktw, st))
            o = o * lax.rsqrt(jnp.mean(o * o, axis=-1, keepdims=True) + GDN_NORM_EPS) * ng_ref[0:1, :]
            o_ref[:, vs(i)] = (o * _silu(z_ref[:, vs(i)])).astype(o_ref.dtype)


def _gdn_chunks(q, k, v, p_main, bg, norm_g, b, s):
    c = min(GDN_CHUNK, s)
    nc = s // c
    kcols = GDN_HB // 2 * GDN_HEAD_DIM
    vcols = GDN_HB * GDN_HEAD_DIM
    z_blk0 = GDN_QKV // vcols
    ng = _stack_rows([norm_g])
    return pl.pallas_call(
        functools.partial(_gdn_chunk_kernel, c=c),
        grid=(b, GDN_GROUPS, nc),
        in_specs=[pl.BlockSpec((c, kcols), lambda bi, g, i: (bi * nc + i, g)),
                  pl.BlockSpec((c, kcols), lambda bi, g, i: (bi * nc + i, g)),
                  pl.BlockSpec((c, vcols), lambda bi, g, i: (bi * nc + i, g)),
                  pl.BlockSpec((c, vcols), lambda bi, g, i: (bi * nc + i, z_blk0 + g)),
                  pl.BlockSpec((1, c, LANES), lambda bi, g, i: (g, bi * nc + i, 0)),
                  pl.BlockSpec(ng.shape, lambda bi, g, i: (0, 0))],
        out_specs=pl.BlockSpec((c, vcols), lambda bi, g, i: (bi * nc + i, g)),
        out_shape=jax.ShapeDtypeStruct((b * s, GDN_VAL_WIDTH), BF16),
        scratch_shapes=[pltpu.VMEM((GDN_HB, GDN_HEAD_DIM, GDN_HEAD_DIM), F32)],
        compiler_params=_params("arbitrary", "arbitrary", "arbitrary"),
        name="gdn_chunks",
    )(q, k, v, p_main, bg, ng)


def _odd_mixer(h_bf, b, s, w_in, conv_w, a_log, dt_bias, norm_g, w_out):
    n_main = GDN_QKV + GDN_VAL_WIDTH
    p_main = _matmul(h_bf, w_in[:, :n_main].astype(BF16), F32)
    w_small = jnp.pad(w_in[:, n_main:], ((0, 0), (0, LANES - 2 * GDN_V_HEADS))).astype(BF16)
    p_small = _matmul(h_bf, w_small, F32)
    q, k, v, bg = _gdn_prep(p_main, p_small, conv_w, a_log, dt_bias, b, s)
    o = _gdn_chunks(q, k, v, p_main, bg, norm_g, b, s)
    return _matmul(o, w_out.astype(BF16), F32, tm=512)


ROUTER_TS = 512
GROUP_SIZE = N_EXPERTS // N_GROUPS


def _beats(other, mine, other_idx, my_idx):
    return jnp.where((other > mine) | ((other == mine) & (other_idx < my_idx)), 1, 0)


def _router_kernel(h_ref, rt_ref, bias_ref, eid_ref, pos_ref, w_ref, cnt_ref, run_ref, *, ts):
    @pl.when(pl.program_id(0) == 0)
    def _():
        run_ref[...] = jnp.zeros_like(run_ref)

    logits = lax.dot_general(rt_ref[...], h_ref[...], (((1,), (1,)), ((), ())),
                             precision=HI, preferred_element_type=F32)
    scores = _sigmoid(logits)
    choice = scores + jnp.tile(bias_ref[...], (1, ts // LANES))
    neg_inf = jnp.float32(-jnp.inf)

    grouped = choice.reshape(N_GROUPS, GROUP_SIZE, ts)
    m1 = jnp.max(grouped, axis=1, keepdims=True)
    ties = jnp.sum(jnp.where(grouped == m1, 1, 0), axis=1, keepdims=True)
    m2 = jnp.max(jnp.where(grouped < m1, grouped, neg_inf), axis=1, keepdims=True)
    group_score = (m1 + jnp.where(ties >= 2, m1, m2)).reshape(N_GROUPS, ts)

    g_idx = lax.broadcasted_iota(jnp.int32, (N_GROUPS, ts), 0)
    g_rank = jnp.zeros((N_GROUPS, ts), jnp.int32)
    for g in range(N_GROUPS):
        g_rank = g_rank + _beats(group_score[g:g + 1, :], group_score, g, g_idx)
    g_keep = jnp.where(g_rank < TOPK_GROUPS, 1.0, 0.0).reshape(N_GROUPS, 1, ts)
    e_keep = jnp.broadcast_to(g_keep, (N_GROUPS, GROUP_SIZE, ts)).reshape(N_EXPERTS, ts)
    masked = jnp.where(e_keep > 0.5, choice, neg_inf)

    e_idx = lax.broadcasted_iota(jnp.int32, (N_EXPERTS, ts), 0)
    e_rank = jnp.zeros((N_EXPERTS, ts), jnp.int32)
    for e in range(N_EXPERTS):
        e_rank = e_rank + _beats(masked[e:e + 1, :], masked, e, e_idx)
    chosen = e_rank < TOP_K
    top_w = jnp.where(chosen, scores, 0.0)
    gates = top_w / jnp.sum(top_w, axis=0, keepdims=True) * ROUTED_SCALE

    picks = jnp.where(chosen, 1.0, 0.0).astype(BF16)
    earlier = jnp.where(lax.broadcasted_iota(jnp.int32, (ts, ts), 0)
                        < lax.broadcasted_iota(jnp.int32, (ts, ts), 1), 1.0, 0.0).astype(BF16)
    pos = jnp.dot(picks, earlier, preferred_element_type=F32) + jnp.tile(run_ref[...], (1, ts // LANES))
    run_ref[...] += jnp.dot(picks, jnp.ones((ts, LANES), BF16), preferred_element_type=F32)
    cnt_ref[...] = run_ref[...]

    e_f = e_idx.astype(F32)
    ids, poss, ws = [], [], []
    for k in range(TOP_K):
        hit = e_rank == k
        ids.append(jnp.sum(jnp.where(hit, e_f, 0.0), axis=0, keepdims=True))
        poss.append(jnp.sum(jnp.where(hit, pos, 0.0), axis=0, keepdims=True))
        ws.append(jnp.sum(jnp.where(hit, gates, 0.0), axis=0, keepdims=True))
    eid_ref[...] = jnp.concatenate(ids, axis=0).astype(jnp.int32)
    pos_ref[...] = jnp.concatenate(poss, axis=0).astype(jnp.int32)
    w_ref[...] = jnp.concatenate(ws, axis=0)


def _router(h, router, router_bias):
    n = h.shape[0]
    ts = min(ROUTER_TS, n)
    rt = router.T.astype(F32)
    bias = jnp.broadcast_to(router_bias.astype(F32)[:, None], (N_EXPERTS, LANES))
    per_tok = pl.BlockSpec((TOP_K, ts), lambda i: (0, i))
    eid, pos, w, cnt = pl.pallas_call(
        functools.partial(_router_kernel, ts=ts),
        grid=(n // ts,),
        in_specs=[pl.BlockSpec((ts, D_MODEL), lambda i: (i, 0)),
                  pl.BlockSpec(rt.shape, lambda i: (0, 0)),
                  pl.BlockSpec(bias.shape, lambda i: (0, 0))],
        out_specs=[per_tok, per_tok, per_tok, pl.BlockSpec((N_EXPERTS, LANES), lambda i: (0, 0))],
        out_shape=[jax.ShapeDtypeStruct((TOP_K, n), jnp.int32), jax.ShapeDtypeStruct((TOP_K, n), jnp.int32),
                   jax.ShapeDtypeStruct((TOP_K, n), F32), jax.ShapeDtypeStruct((N_EXPERTS, LANES), F32)],
        scratch_shapes=[pltpu.VMEM((N_EXPERTS, LANES), F32)],
        compiler_params=_params("arbitrary"),
        name="router",
    )(h, rt, bias)
    return eid, pos, w, cnt[:, 0].astype(jnp.int32)


MOE_TM = 256
MOE_TS = 256


def _swiglu_hidden(x, w_gu, ff):
    gu = jnp.dot(x, w_gu, preferred_element_type=F32)
    return _silu(gu[:, :ff]) * gu[:, ff:]


def _tile_copy(src, src_row, dst, dst_row, sem):
    return pltpu.make_async_copy(src.at[pl.ds(pl.multiple_of(src_row, PACK_ROWS), PACK_ROWS), :],
                                 dst.at[pl.ds(pl.multiple_of(dst_row, PACK_ROWS), PACK_ROWS), :], sem)


def _dispatch_kernel(slots_ref, hp_ref, xs_in_ref, xs_ref, sem, *, ts):
    del xs_in_ref

    def issue(t, carry):
        for k in range(TOP_K):
            slot = slots_ref[0, 0, t * TOP_K + k]
            _tile_copy(hp_ref, t * PACK_ROWS, xs_ref, slot * PACK_ROWS, sem).start()
        return carry

    lax.fori_loop(0, ts, issue, 0)
    for _ in range(TOP_K):
        pltpu.make_async_copy(hp_ref, xs_ref.at[pl.ds(0, ts * PACK_ROWS), :], sem).wait()


def _dispatch(hp, slots, n_slots):
    n = hp.shape[0] // PACK_ROWS
    ts = min(MOE_TS, n)
    xs0 = jnp.zeros((n_slots * PACK_ROWS, LANES), jnp.uint32)
    return pl.pallas_call(
        functools.partial(_dispatch_kernel, ts=ts),
        grid=(n // ts,),
        in_specs=[pl.BlockSpec((1, 1, ts * TOP_K), lambda i: (i, 0, 0), memory_space=pltpu.SMEM),
                  pl.BlockSpec((ts * PACK_ROWS, LANES), lambda i: (i, 0)),
                  pl.BlockSpec(memory_space=pl.ANY)],
        out_specs=pl.BlockSpec(memory_space=pl.ANY),
        out_shape=jax.ShapeDtypeStruct(xs0.shape, jnp.uint32),
        scratch_shapes=[pltpu.SemaphoreType.DMA(())],
        input_output_aliases={2: 0},
        compiler_params=_params("arbitrary"),
        name="moe_dispatch",
    )(slots, hp, xs0)


def _experts_kernel(te_ref, nv_ref, xs_ref, gu_ref, dn_ref, ys_ref):
    del te_ref
    live = pl.program_id(0) < nv_ref[0]

    @pl.when(live)
    def _():
        x = _load_packed(xs_ref, 0, MOE_TM)
        hid = _swiglu_hidden(x, gu_ref[0], EXPERT_FF)
        y = jnp.dot(hid.astype(BF16), dn_ref[0], preferred_element_type=F32)
        _store_packed(ys_ref, y, MOE_TM)

    @pl.when(jnp.logical_not(live))
    def _():
        ys_ref[...] = jnp.zeros_like(ys_ref)


def _experts(xs, tile_expert, n_valid, e_gu, e_dn):
    n_tiles = tile_expert.shape[0]
    rows = MOE_TM * PACK_ROWS
    return pl.pallas_call(
        _experts_kernel,
        grid_spec=pltpu.PrefetchScalarGridSpec(
            num_scalar_prefetch=2,
            grid=(n_tiles,),
            in_specs=[pl.BlockSpec((rows, LANES), lambda t, te, nv: (t, 0)),
                      pl.BlockSpec((1,) + e_gu.shape[1:], lambda t, te, nv: (te[t], 0, 0)),
                      pl.BlockSpec((1,) + e_dn.shape[1:], lambda t, te, nv: (te[t], 0, 0))],
            out_specs=pl.BlockSpec((rows, LANES), lambda t, te, nv: (t, 0))),
        out_shape=jax.ShapeDtypeStruct(xs.shape, jnp.uint32),
        compiler_params=_params("arbitrary"),
        name="moe_experts",
    )(tile_expert, n_valid, xs, e_gu, e_dn)


def _combine_kernel(slots_ref, w_ref, hb_ref, h_ref, ys_ref, sgu_ref, sdn_ref, g_ref, b_ref,
                    o_ref, ob_ref, ybuf, sem, *, ts):
    def issue(t, carry):
        for k in range(TOP_K):
            slot = slots_ref[0, 0, t * TOP_K + k]
            _tile_copy(ys_ref, slot * PACK_ROWS, ybuf, (k * ts + t) * PACK_ROWS, sem).start()
        return carry

    lax.fori_loop(0, ts, issue, 0)
    hid = _swiglu_hidden(hb_ref[...], sgu_ref[...], SHARED_FF)
    y = jnp.dot(hid.astype(BF16), sdn_ref[...], preferred_element_type=F32)
    for _ in range(TOP_K):
        pltpu.make_async_copy(ys_ref.at[pl.ds(0, ts * PACK_ROWS), :], ybuf.at[pl.ds(0, ts * PACK_ROWS), :], sem).wait()
    for k in range(TOP_K):
        y = y + w_ref[:, k:k + 1] * _load_packed(ybuf, k * ts * PACK_ROWS, ts).astype(F32)
    out = _ln_rows(DN_ALPHA * h_ref[...] + y, g_ref[...], b_ref[...])
    o_ref[...] = out
    ob_ref[...] = out.astype(BF16)


def _combine(h, h_bf, ys, slots, w_tok, s_gu, s_dn, ln_g, ln_b):
    n, d = h.shape
    ts = min(MOE_TS, n)
    row = lambda c: pl.BlockSpec((ts, c), lambda i: (i, 0))
    full = lambda a: pl.BlockSpec(a.shape, lambda i: (0,) * a.ndim)
    g2, b2 = ln_g.reshape(1, d), ln_b.reshape(1, d)
    return pl.pallas_call(
        functools.partial(_combine_kernel, ts=ts),
        grid=(n // ts,),
        in_specs=[pl.BlockSpec((1, 1, ts * TOP_K), lambda i: (i, 0, 0), memory_space=pltpu.SMEM),
                  row(TOP_K), row(d), row(d), pl.BlockSpec(memory_space=pl.ANY),
                  full(s_gu), full(s_dn), full(g2), full(b2)],
        out_specs=[row(d), row(d)],
        out_shape=[jax.ShapeDtypeStruct((n, d), F32), jax.ShapeDtypeStruct((n, d), BF16)],
        scratch_shapes=[pltpu.VMEM((TOP_K * ts * PACK_ROWS, LANES), jnp.uint32), pltpu.SemaphoreType.DMA(())],
        compiler_params=_params("arbitrary"),
        name="moe_combine",
    )(slots, w_tok, h_bf, h, ys, s_gu, s_dn, g2, b2)


def _ffn(h, h_bf, hp, router, router_bias, e_gate, e_up, e_down, s_gate, s_up, s_down, ln_g, ln_b):
    n = h.shape[0]
    eid, pos, w, counts = _router(h, router, router_bias)
    padded = (counts + MOE_TM - 1) // MOE_TM * MOE_TM
    ends = jnp.cumsum(padded)
    starts = ends - padded
    n_slots = n * TOP_K + N_EXPERTS * MOE_TM
    n_tiles = n_slots // MOE_TM
    ts = min(MOE_TS, n)
    slots = (starts[eid] + pos).T.reshape(n // ts, 1, ts * TOP_K)
    tile_expert = jnp.minimum(jnp.searchsorted(ends, jnp.arange(n_tiles, dtype=jnp.int32) * MOE_TM, side="right"),
                              N_EXPERTS - 1).astype(jnp.int32)
    n_valid = (ends[-1:] // MOE_TM).astype(jnp.int32)

    e_gu = jnp.concatenate([e_gate, e_up], axis=2).astype(BF16)
    s_gu = jnp.concatenate([s_gate, s_up], axis=1).astype(BF16)
    xs = _dispatch(hp, slots, n_slots)
    ys = _experts(xs, tile_expert, n_valid, e_gu, e_down.astype(BF16))
    return _combine(h, h_bf, ys, slots, w.T, s_gu, s_down.astype(BF16), ln_g, ln_b)


def _trunk(x, layers):
    b, s, d = x.shape
    n = b * s
    x = x.reshape(n, d)
    x_bf = x.astype(BF16)
    v_first = None
    for i, (mixer, ffn) in enumerate(layers):
        if i % 2 == 0:
            mix, v_first = _even_mixer(x_bf, v_first, b, s, *mixer)
        else:
            mix = _odd_mixer(x_bf, b, s, *mixer)
        h, h_bf, hp = _add_ln(x, mix, ffn[0], ffn[1])
        x, x_bf = _ffn(h, h_bf, hp, *ffn[2:])
    return x.reshape(b, s, d)


def kernel(x, l0_w_in, l0_mu, l0_w_up, l0_w0, l0_a_up, l0_a0, l0_g_up, l0_k_k, l0_k_a, l0_r_k, l0_lnx_g, l0_lnx_b, l0_w_out, l0_ln1_g, l0_ln1_b, l0_router, l0_router_bias, l0_e_gate, l0_e_up, l0_e_down, l0_s_gate, l0_s_up, l0_s_down, l0_ln2_g, l0_ln2_b, l1_w_in, l1_conv, l1_a_log, l1_dt_bias, l1_norm_g, l1_w_out, l1_ln1_g, l1_ln1_b, l1_router, l1_router_bias, l1_e_gate, l1_e_up, l1_e_down, l1_s_gate, l1_s_up, l1_s_down, l1_ln2_g, l1_ln2_b, l2_w_in, l2_mu, l2_w_up, l2_w0, l2_a_up, l2_a0, l2_v_up, l2_v0, l2_g_up, l2_k_k, l2_k_a, l2_r_k, l2_lnx_g, l2_lnx_b, l2_w_out, l2_ln1_g, l2_ln1_b, l2_router, l2_router_bias, l2_e_gate, l2_e_up, l2_e_down, l2_s_gate, l2_s_up, l2_s_down, l2_ln2_g, l2_ln2_b, l3_w_in, l3_conv, l3_a_log, l3_dt_bias, l3_norm_g, l3_w_out, l3_ln1_g, l3_ln1_b, l3_router, l3_router_bias, l3_e_gate, l3_e_up, l3_e_down, l3_s_gate, l3_s_up, l3_s_down, l3_ln2_g, l3_ln2_b):
    layers = [
        ((l0_w_in, l0_mu, l0_w_up, l0_w0, l0_a_up, l0_a0, None, None, l0_g_up, l0_k_k, l0_k_a, l0_r_k,
          l0_lnx_g, l0_lnx_b, l0_w_out),
         (l0_ln1_g, l0_ln1_b, l0_router, l0_router_bias, l0_e_gate, l0_e_up, l0_e_down,
          l0_s_gate, l0_s_up, l0_s_down, l0_ln2_g, l0_ln2_b)),
        ((l1_w_in, l1_conv, l1_a_log, l1_dt_bias, l1_norm_g, l1_w_out),
         (l1_ln1_g, l1_ln1_b, l1_router, l1_router_bias, l1_e_gate, l1_e_up, l1_e_down,
          l1_s_gate, l1_s_up, l1_s_down, l1_ln2_g, l1_ln2_b)),
        ((l2_w_in, l2_mu, l2_w_up, l2_w0, l2_a_up, l2_a0, l2_v_up, l2_v0, l2_g_up, l2_k_k, l2_k_a, l2_r_k,
          l2_lnx_g, l2_lnx_b, l2_w_out),
         (l2_ln1_g, l2_ln1_b, l2_router, l2_router_bias, l2_e_gate, l2_e_up, l2_e_down,
          l2_s_gate, l2_s_up, l2_s_down, l2_ln2_g, l2_ln2_b)),
        ((l3_w_in, l3_conv, l3_a_log, l3_dt_bias, l3_norm_g, l3_w_out),
         (l3_ln1_g, l3_ln1_b, l3_router, l3_router_bias, l3_e_gate, l3_e_up, l3_e_down,
          l3_s_gate, l3_s_up, l3_s_down, l3_ln2_g, l3_ln2_b)),
    ]
    return _trunk(x, layers)
```

```python
import functools
import math

import jax
import jax.numpy as jnp
from jax import lax
from jax.experimental import pallas as pl
from jax.experimental.pallas import tpu as pltpu

D_MODEL = 2048
DEPTH = 4
SB_HEADS = 16
SB_HEAD_DIM = 64
SB_WIDTH = SB_HEADS * SB_HEAD_DIM
RW_HEADS = 16
RW_HEAD_DIM = 64
RW_WIDTH = RW_HEADS * RW_HEAD_DIM
RW_DECAY_RANK = 96
RW_ICLR_RANK = 96
RW_VRES_RANK = 64
RW_GATE_RANK = 256
RW_GN_EPS = 64e-5
GDN_QK_HEADS = 16
GDN_V_HEADS = 32
GDN_HEAD_DIM = 128
GDN_KEY_WIDTH = GDN_QK_HEADS * GDN_HEAD_DIM
GDN_VAL_WIDTH = GDN_V_HEADS * GDN_HEAD_DIM
GDN_CONV = 4
GDN_NORM_EPS = 1e-6
N_EXPERTS = 64
TOP_K = 8
N_GROUPS = 8
TOPK_GROUPS = 4
EXPERT_FF = 384
SHARED_FF = 384
ROUTED_SCALE = 2.5
DN_ALPHA = (2 * DEPTH) ** 0.25
LN_EPS = 1e-5
L2_EPS = 1e-6

LANES = 128
VMEM_LIMIT = 56 * 1024 * 1024
F32 = jnp.float32
BF16 = jnp.bfloat16
HI = lax.Precision.HIGHEST


def _params(*sem):
    return pltpu.CompilerParams(dimension_semantics=sem, vmem_limit_bytes=VMEM_LIMIT)


def _contract(a, b, dims):
    return lax.dot_general(a.astype(BF16), b.astype(BF16), (dims, ((), ())), preferred_element_type=F32)


def _dot(a, b):
    return _contract(a, b, ((1,), (0,)))


def _dot_nt(a, b):
    return _contract(a, b, ((1,), (1,)))


def _dot_tn(a, b):
    return _contract(a, b, ((0,), (0,)))


def _sigmoid(x):
    return 1.0 / (1.0 + jnp.exp(-x))


def _softplus(x):
    return jnp.maximum(x, 0.0) + jnp.log(1.0 + jnp.exp(-jnp.abs(x)))


def _silu(x):
    return x * _sigmoid(x)


def _matmul_kernel(x_ref, w_ref, o_ref, wb_ref):
    @pl.when(pl.program_id(1) == 0)
    def _():
        wb_ref[...] = w_ref[...].astype(BF16)

    o_ref[...] = jnp.dot(x_ref[...], wb_ref[...], preferred_element_type=F32).astype(o_ref.dtype)


def _matmul(x, w, out_dtype, tm=1024, tn=1024, m=None):
    n, k = x.shape
    m = w.shape[1] if m is None else m
    tm = min(tm, n)
    tn = min(tn, m)
    assert n % tm == 0 and m % tn == 0, (n, m, tm, tn)
    return pl.pallas_call(
        _matmul_kernel,
        grid=(m // tn, n // tm),
        in_specs=[pl.BlockSpec((tm, k), lambda j, i: (i, 0)),
                  pl.BlockSpec((k, tn), lambda j, i: (0, j))],
        out_specs=pl.BlockSpec((tm, tn), lambda j, i: (i, j)),
        out_shape=jax.ShapeDtypeStruct((n, m), out_dtype),
        scratch_shapes=[pltpu.VMEM((k, tn), BF16)],
        compiler_params=_params("arbitrary", "arbitrary"),
        name="matmul",
    )(x, w)


def _ln_rows(y, g, b):
    mean = jnp.mean(y, axis=-1, keepdims=True)
    yc = y - mean
    var = jnp.mean(yc * yc, axis=-1, keepdims=True)
    return yc * lax.rsqrt(var + LN_EPS) * g + b


PACK_ROWS = D_MODEL // 2 // LANES


def _bf16_bits(x):
    return pltpu.bitcast(x.astype(BF16).astype(F32), jnp.uint32)


def _store_packed(ref, x, rows):
    half = D_MODEL // 2
    words = (_bf16_bits(x[:, half:]) & jnp.uint32(0xFFFF0000)) | (_bf16_bits(x[:, :half]) >> 16)
    for s in range(PACK_ROWS):
        ref[pl.ds(s, rows, stride=PACK_ROWS), :] = words[:, s * LANES:(s + 1) * LANES]


def _load_packed(ref, start, rows):
    lo, hi = [], []
    for s in range(PACK_ROWS):
        words = ref[pl.ds(start + s, rows, stride=PACK_ROWS), :]
        lo.append(pltpu.bitcast(words << 16, F32).astype(BF16))
        hi.append(pltpu.bitcast(words & jnp.uint32(0xFFFF0000), F32).astype(BF16))
    return jnp.concatenate(lo + hi, axis=1)


def _add_ln_kernel(x_ref, m_ref, g_ref, b_ref, o_ref, ob_ref, op_ref, *, tm):
    y = DN_ALPHA * x_ref[...] + m_ref[...].astype(F32)
    out = _ln_rows(y, g_ref[...], b_ref[...])
    o_ref[...] = out
    ob_ref[...] = out.astype(BF16)
    _store_packed(op_ref, out, tm)


def _add_ln(x, mix, g, b, tm=512):
    n, d = x.shape
    tm = min(tm, n)
    row = pl.BlockSpec((tm, d), lambda i: (i, 0))
    vec = pl.BlockSpec((1, d), lambda i: (0, 0))
    return pl.pallas_call(
        functools.partial(_add_ln_kernel, tm=tm),
        grid=(n // tm,),
        in_specs=[row, row, vec, vec],
        out_specs=[row, row, pl.BlockSpec((tm * PACK_ROWS, LANES), lambda i: (i, 0))],
        out_shape=[jax.ShapeDtypeStruct((n, d), F32), jax.ShapeDtypeStruct((n, d), BF16),
                   jax.ShapeDtypeStruct((n * PACK_ROWS, LANES), jnp.uint32)],
        compiler_params=_params("arbitrary"),
        name="add_ln",
    )(x, mix, g.reshape(1, d), b.reshape(1, d))


SB_TQ = 256
SB_TK = 128
SB_UNDERFLOW = -104.0


def _sb_kernel(q_ref, k_ref, v_ref, o_ref, *, tq, tk):
    i = pl.program_id(2)
    q = q_ref[0] * jnp.asarray(SB_HEAD_DIM ** -0.5, BF16)
    lane = lax.broadcasted_iota(jnp.int32, (tq, LANES), 1)
    zero = jnp.zeros_like(q)
    q_heads = (jnp.where(lane < SB_HEAD_DIM, q, zero), jnp.where(lane >= SB_HEAD_DIM, q, zero))
    q_pos = i * tq + lax.broadcasted_iota(jnp.int32, (tq, tk), 0)
    k_off = lax.broadcasted_iota(jnp.int32, (tq, tk), 1)
    r2 = lax.broadcasted_iota(jnp.int32, (2 * tk, tk + LANES), 0) % tk
    c2 = lax.broadcasted_iota(jnp.int32, (2 * tk, tk + LANES), 1)
    cum = jnp.where((r2 > c2) | (c2 >= tk), 1.0, 0.0).astype(BF16)
    n_diag = tq // tk
    first_diag = i * n_diag

    def block(j, carry, on_diagonal):
        start = pl.multiple_of(j * tk, tk)
        ks = k_ref[0, pl.ds(start, tk), :]
        vs = v_ref[0, pl.ds(start, tk), :]
        causal = (start + k_off) < q_pos
        new = []
        for h in range(2):
            acc, later_blocks = carry[2 * h], carry[2 * h + 1]
            z = _dot_nt(q_heads[h], ks)
            sp = _softplus(z)
            log_keep = jnp.where(causal, -sp, 0.0) if on_diagonal else -sp
            hi = log_keep.astype(BF16)
            lo = (log_keep - hi.astype(F32)).astype(BF16)
            sums = jnp.dot(jnp.concatenate([hi, lo], axis=1), cum, preferred_element_type=F32)
            later = sums[:, :tk] + jnp.tile(later_blocks, (1, tk // LANES))
            w = jnp.exp(z - sp + later)
            if on_diagonal:
                w = jnp.where(causal, w, 0.0)
            acc = acc + jnp.dot(w.astype(BF16), vs, preferred_element_type=F32)
            new += [acc, later_blocks + sums[:, tk:]]
        return tuple(new)

    carry = tuple(jnp.zeros((tq, LANES), F32) for _ in range(4))
    for d in range(n_diag):
        carry = block(first_diag + n_diag - 1 - d, carry, True)

    def live(carry):
        return jnp.max(jnp.maximum(carry[1], carry[3])) > SB_UNDERFLOW

    def cond(state):
        return (state[0] >= 0) & state[1]

    def body(state):
        carry = block(state[0], state[2], False)
        return state[0] - 1, live(carry), carry

    _, _, res = lax.while_loop(cond, body, (first_diag - 1, live(carry), carry))
    o_ref[0] = jnp.where(lane < SB_HEAD_DIM, res[0], res[2]).astype(o_ref.dtype)


def _stick_breaking(qkv, b, s):
    tq = min(SB_TQ, s)
    tk = min(SB_TK, s)
    pairs = SB_WIDTH // LANES
    return pl.pallas_call(
        functools.partial(_sb_kernel, tq=tq, tk=tk),
        grid=(b, pairs, s // tq),
        in_specs=[pl.BlockSpec((1, tq, LANES), lambda bi, p, i: (bi, i, p)),
                  pl.BlockSpec((1, s, LANES), lambda bi, p, i: (bi, 0, pairs + p)),
                  pl.BlockSpec((1, s, LANES), lambda bi, p, i: (bi, 0, 2 * pairs + p))],
        out_specs=pl.BlockSpec((1, tq, LANES), lambda bi, p, i: (bi, i, p)),
        out_shape=jax.ShapeDtypeStruct((b, s, SB_WIDTH), BF16),
        compiler_params=_params("arbitrary", "arbitrary", "arbitrary"),
        name="stick_breaking",
    )(qkv, qkv, qkv)


def _split3(x):
    hi = x.astype(BF16)
    r1 = x - hi.astype(F32)
    mid = r1.astype(BF16)
    lo = (r1 - mid.astype(F32)).astype(BF16)
    return hi, mid, lo


def _dot_sel_rhs(x, sel):
    return sum(jnp.dot(p, sel, preferred_element_type=F32) for p in _split3(x))


def _dot_sel_lhs(sel, x):
    return sum(jnp.dot(sel, p, preferred_element_type=F32) for p in _split3(x))


def _head_sum_matrices(width, head_dim):
    heads = width // head_dim
    assert heads <= LANES
    c = lax.broadcasted_iota(jnp.int32, (width, LANES), 0) // head_dim
    h = lax.broadcasted_iota(jnp.int32, (width, LANES), 1)
    gather = jnp.where(c == h, 1.0, 0.0).astype(BF16)
    ct = lax.broadcasted_iota(jnp.int32, (LANES, width), 1) // head_dim
    ht = lax.broadcasted_iota(jnp.int32, (LANES, width), 0)
    spread = jnp.where(ct == ht, 1.0, 0.0).astype(BF16)
    return gather, spread


def _per_head_sum(x, gather, spread):
    return _dot_sel_rhs(_dot_sel_rhs(x, gather), spread)


def _neumann_inverse(a, size, nilpotent=None):
    nilpotent = size if nilpotent is None else nilpotent
    eye = jnp.where(lax.broadcasted_iota(jnp.int32, (size, size), 0)
                    == lax.broadcasted_iota(jnp.int32, (size, size), 1), 1.0, 0.0)
    inv = eye + a
    power = a
    span = 2
    while span < nilpotent:
        power = _dot(power, power)
        inv = inv + _dot(inv, power)
        span *= 2
    return inv


RW_LORA = 512
RW_COLS = 3 * RW_WIDTH + RW_LORA
RW_TS = 256
RW_CHUNK = 64
RW_STACK = 2


def _rwkv_prep_kernel(pb_ref, mu_ref, wup_ref, aup_ref, gup_ref, vup_ref, vec_ref, vfirst_ref,
                      r_ref, k_ref, v_ref, kk_ref, a_ref, lw_ref, g_ref, bonus_ref,
                      last_ref, *, ts, has_vres):
    @pl.when(pl.program_id(1) == 0)
    def _():
        last_ref[...] = jnp.zeros_like(last_ref)

    x = pb_ref[...]
    prev = pltpu.roll(x, 1, 0)
    first_row = lax.broadcasted_iota(jnp.int32, x.shape, 0) == 0
    prev = jnp.where(first_row, jnp.broadcast_to(last_ref[0:1, :], x.shape), prev)
    last_ref[0:1, :] = x[ts - 1:ts, :]
    x = x + (prev - x) * mu_ref[...]

    w = RW_WIDTH
    r, k, v, lora = x[:, :w], x[:, w:2 * w], x[:, 2 * w:3 * w], x[:, 3 * w:]
    w0, a0, v0, k_k, k_a, r_k = (vec_ref[i:i + 1, :] for i in range(6))
    w_log = -_softplus(-(w0 + _dot(jnp.tanh(lora), wup_ref[...]))) - 0.5
    lw_ref[...] = -jnp.exp(w_log)
    if has_vres:
        v = v + (vfirst_ref[...] - v) * _sigmoid(v0 + _dot(lora, vup_ref[...]))
    a = _sigmoid(a0 + _dot(lora, aup_ref[...]))
    g_ref[...] = _dot(_sigmoid(lora), gup_ref[...])
    gather, spread = _head_sum_matrices(w, RW_HEAD_DIM)
    kk = k * k_k
    kk = kk * lax.rsqrt(_per_head_sum(kk * kk, gather, spread) + L2_EPS)
    k = k * (1.0 + (a - 1.0) * k_a)
    bonus_ref[...] = _per_head_sum(r * k * r_k, gather, spread) * v
    r_ref[...] = r
    k_ref[...] = k
    v_ref[...] = v
    kk_ref[...] = kk
    a_ref[...] = a


def _rwkv_prep(pb, b, s, mu, wup, aup, gup, vup, vecs, v_first):
    n = b * s
    ts = min(RW_TS, s)
    nt = s // ts
    has_vres = v_first is not None
    if not has_vres:
        v_first = jnp.zeros((8, RW_WIDTH), F32)
        vf_spec = pl.BlockSpec((8, RW_WIDTH), lambda bi, i: (0, 0))
    else:
        vf_spec = pl.BlockSpec((ts, RW_WIDTH), lambda bi, i: (bi * nt + i, 0))
    row = lambda c: pl.BlockSpec((ts, c), lambda bi, i: (bi * nt + i, 0))
    full = lambda a: pl.BlockSpec(a.shape, lambda bi, i: (0, 0))
    outs = [jax.ShapeDtypeStruct((n, RW_WIDTH), F32)] * 8
    return pl.pallas_call(
        functools.partial(_rwkv_prep_kernel, ts=ts, has_vres=has_vres),
        grid=(b, nt),
        in_specs=[row(RW_COLS), full(mu), full(wup), full(aup), full(gup), full(vup), full(vecs), vf_spec],
        out_specs=[row(RW_WIDTH)] * 8,
        out_shape=outs,
        scratch_shapes=[pltpu.VMEM((8, RW_COLS), F32)],
        compiler_params=_params("arbitrary", "arbitrary"),
        name="rwkv_prep",
    )(pb, mu, wup, aup, gup, vup, vecs, v_first)


def _rwkv_chunk_kernel(r_ref, k_ref, v_ref, kk_ref, a_ref, lw_ref, g_ref, bonus_ref, lnx_ref,
                       o_ref, state_ref, *, c):
    @pl.when(pl.program_id(1) == 0)
    def _():
        state_ref[...] = jnp.zeros_like(state_ref)

    tri = jnp.where(lax.broadcasted_iota(jnp.int32, (c, c), 0)
                    >= lax.broadcasted_iota(jnp.int32, (c, c), 1), 1.0, 0.0).astype(BF16)

    lw = lw_ref[...]
    cum = _dot_sel_lhs(tri, lw)
    total = cum[c - 1:c, :]
    kk = kk_ref[...]
    b_vec = kk * a_ref[...]
    dec_out = jnp.exp(total - cum)
    inv_cum = jnp.exp(-cum)
    r_in = r_ref[...] * jnp.exp(cum)
    a_in = -kk * jnp.exp(cum - lw)
    b_out = b_vec * inv_cum
    k_out = k_ref[...] * inv_cum
    b_end = b_vec * dec_out
    k_end = k_ref[...] * dec_out
    v_all = v_ref[...]
    ones = jnp.ones((c, LANES), BF16)
    p_end = jnp.exp(sum(lax.dot_general(piece, ones, (((0,), (0,)), ((), ())), preferred_element_type=F32)
                        for piece in _split3(lw)))

    lane = lax.broadcasted_iota(jnp.int32, (c, LANES), 1)
    first = lane < RW_HEAD_DIM
    blk_r = lax.broadcasted_iota(jnp.int32, (LANES, LANES), 0)
    blk_c = lax.broadcasted_iota(jnp.int32, (LANES, LANES), 1)
    same_head = (blk_r // RW_HEAD_DIM) == (blk_c // RW_HEAD_DIM)

    rows = 2 * RW_STACK * c
    srr = lax.broadcasted_iota(jnp.int32, (rows, rows), 0)
    scc = lax.broadcasted_iota(jnp.int32, (rows, rows), 1)
    same_blk = (srr // c) == (scc // c)
    strict = same_blk & (srr > scc)
    incl = same_blk & (srr >= scc)
    stack = lambda parts: jnp.concatenate(parts, axis=0)
    twice = lambda xs: stack([x for x in xs for _ in range(2)])
    by_head = lambda xs: stack([jnp.where(keep, x, 0.0) for x in xs for keep in (first, ~first)])
    pick = lambda x, j: jnp.where(first, x[2 * j * c:(2 * j + 1) * c], x[(2 * j + 1) * c:(2 * j + 2) * c])

    ys = []
    for grp in range(RW_WIDTH // LANES // RW_STACK):
        sls = [slice(p * LANES, (p + 1) * LANES) for p in range(grp * RW_STACK, (grp + 1) * RW_STACK)]
        ai, ri = [a_in[:, sl] for sl in sls], [r_in[:, sl] for sl in sls]
        bo, ko, v = twice([b_out[:, sl] for sl in sls]), twice([k_out[:, sl] for sl in sls]), twice([v_all[:, sl] for sl in sls])
        ai_h, ri_h = by_head(ai), by_head(ri)
        a_ab = jnp.where(strict, _dot_nt(ai_h, bo), 0.0)
        a_ak = jnp.where(strict, _dot_nt(ai_h, ko), 0.0)
        a_rb = jnp.where(incl, _dot_nt(ri_h, bo), 0.0)
        a_rk = jnp.where(incl, _dot_nt(ri_h, ko), 0.0)
        t_inv = _neumann_inverse(a_ab, rows, nilpotent=c)
        w_rows = _dot(t_inv, twice(ai))
        u_rows = _dot(t_inv, _dot(a_ak, v))
        w_t = [pick(w_rows, j) for j in range(RW_STACK)]
        u0 = [pick(u_rows, j) for j in range(RW_STACK)]
        q_rows = _dot(a_rb, twice(w_t))
        y_rows = _dot(a_rb, twice(u0)) + _dot(a_rk, v)
        for j, sl in enumerate(sls):
            p = grp * RW_STACK + j
            be, ke, vp = b_end[:, sl], k_end[:, sl], v_all[:, sl]
            m_t = jnp.where(same_head, _dot_tn(be, w_t[j]), 0.0)
            n_t = jnp.where(same_head, _dot_tn(be, u0[j]) + _dot_tn(ke, vp), 0.0)
            st = state_ref[p]
            ys.append(_dot(ri[j] + pick(q_rows, j), st) + pick(y_rows, j))
            state_ref[p] = p_end[sl, :] * st + (_dot(m_t, st) + n_t)
    y = jnp.concatenate(ys, axis=1)

    gather, spread = _head_sum_matrices(RW_WIDTH, RW_HEAD_DIM)
    mean = _per_head_sum(y, gather, spread) * (1.0 / RW_HEAD_DIM)
    yc = y - mean
    var = _per_head_sum(yc * yc, gather, spread) * (1.0 / RW_HEAD_DIM)
    y = yc * lax.rsqrt(var + RW_GN_EPS) * lnx_ref[0:1, :] + lnx_ref[1:2, :]
    o_ref[...] = ((y + bonus_ref[...]) * g_ref[...]).astype(o_ref.dtype)


def _rwkv_chunks(parts, lnx, b, s):
    c = min(RW_CHUNK, s)
    nc = s // c
    row = pl.BlockSpec((c, RW_WIDTH), lambda bi, i: (bi * nc + i, 0))
    return pl.pallas_call(
        functools.partial(_rwkv_chunk_kernel, c=c),
        grid=(b, nc),
        in_specs=[row] * 8 + [pl.BlockSpec(lnx.shape, lambda bi, i: (0, 0))],
        out_specs=row,
        out_shape=jax.ShapeDtypeStruct((b * s, RW_WIDTH), BF16),
        scratch_shapes=[pltpu.VMEM((RW_WIDTH // LANES, LANES, LANES), F32)],
        compiler_params=_params("arbitrary", "arbitrary"),
        name="rwkv_chunks",
    )(*parts, lnx)


def _pad_rows(w, start, total):
    return jnp.zeros((total, w.shape[1]), w.dtype).at[start:start + w.shape[0]].set(w)


def _stack_rows(vectors, rows=8):
    mat = jnp.stack([v.reshape(-1).astype(F32) for v in vectors])
    return jnp.pad(mat, ((0, rows - mat.shape[0]), (0, 0)))


def _even_mixer(h_bf, v_first, b, s, w_in, mu, w_up, w0, a_up, a0, v_up, v0, g_up,
                k_k, k_a, r_k, lnx_g, lnx_b, w_out):
    n = b * s
    n_a = 3 * SB_WIDTH
    n_b = w_in.shape[1] - n_a
    qkv = _matmul(h_bf, w_in, BF16, m=n_a)
    w_b = jnp.pad(w_in[:, n_a:], ((0, 0), (0, RW_COLS - n_b)))
    pb = _matmul(h_bf, w_b, F32, tn=RW_COLS // 4)
    y_a = _stick_breaking(qkv.reshape(b, s, n_a), b, s).reshape(n, SB_WIDTH)

    o_w, o_a, o_g, o_v = 0, RW_DECAY_RANK, RW_DECAY_RANK + RW_ICLR_RANK, RW_DECAY_RANK + RW_ICLR_RANK + RW_GATE_RANK
    wup = _pad_rows(w_up, o_w, RW_LORA).astype(BF16)
    aup = _pad_rows(a_up, o_a, RW_LORA).astype(BF16)
    gup = _pad_rows(g_up, o_g, RW_LORA).astype(BF16)
    if v_up is None:
        vup = jnp.zeros((RW_LORA, RW_WIDTH), BF16)
        v0 = jnp.zeros((RW_WIDTH,), F32)
    else:
        vup = _pad_rows(v_up, o_v, RW_LORA).astype(BF16)
    mu_p = jnp.pad(mu, (0, RW_COLS - n_b)).reshape(1, RW_COLS)
    vecs = _stack_rows([w0, a0, v0, k_k, k_a, r_k])
    parts = _rwkv_prep(pb, b, s, mu_p, wup, aup, gup, vup, vecs, v_first)
    if v_first is None:
        v_first = parts[2]
    y_b = _rwkv_chunks(parts, _stack_rows([lnx_g, lnx_b]), b, s)
    y = jnp.concatenate([y_a, y_b], axis=1)
    return _matmul(y, w_out, F32), v_first


GDN_QKV = 2 * GDN_KEY_WIDTH + GDN_VAL_WIDTH
GDN_TS = 256
GDN_CHUNK = 64
GDN_HB = 8
GDN_STACK = 4
GDN_GROUPS = GDN_V_HEADS // GDN_HB
GDN_G_LANE = 64


def _gdn_prep_kernel(p_ref, s_ref, conv_ref, vec_ref, q_ref, k_ref, v_ref, bg_ref, last_ref, *, ts):
    @pl.when(pl.program_id(1) == 0)
    def _():
        last_ref[...] = jnp.zeros_like(last_ref)

    x = p_ref[...]
    last = last_ref[...]
    row8 = lax.broadcasted_iota(jnp.int32, (8, GDN_QKV), 0)
    acc = x * conv_ref[GDN_CONV - 1:GDN_CONV, :]
    for d in range(1, GDN_CONV):
        sh = pltpu.roll(x, d, 0)
        top = jnp.where(row8 < d, pltpu.roll(last, d, 0), sh[:8])
        sh = jnp.concatenate([top, sh[8:]], axis=0)
        acc = acc + sh * conv_ref[GDN_CONV - 1 - d:GDN_CONV - d, :]
    last_ref[...] = x[ts - 8:, :]
    y = _silu(acc)

    kw = GDN_KEY_WIDTH
    for h in range(GDN_QK_HEADS):
        sl = slice(h * GDN_HEAD_DIM, (h + 1) * GDN_HEAD_DIM)
        qh = y[:, sl]
        q_ref[:, sl] = qh * (lax.rsqrt(jnp.sum(qh * qh, axis=-1, keepdims=True) + L2_EPS) * GDN_HEAD_DIM ** -0.5)
        kh = y[:, kw + h * GDN_HEAD_DIM:kw + (h + 1) * GDN_HEAD_DIM]
        k_ref[:, sl] = kh * lax.rsqrt(jnp.sum(kh * kh, axis=-1, keepdims=True) + L2_EPS)
    v_ref[...] = y[:, 2 * kw:]

    small = s_ref[...]
    lane = lax.broadcasted_iota(jnp.int32, small.shape, 1)
    neg_a = vec_ref[0:1, :]
    dt_b = vec_ref[1:2, :]
    vals = jnp.where(lane < GDN_V_HEADS, _sigmoid(small), neg_a * _softplus(small + dt_b))
    src = lax.broadcasted_iota(jnp.int32, (LANES, LANES), 0)
    dst = lax.broadcasted_iota(jnp.int32, (LANES, LANES), 1)
    for grp in range(GDN_GROUPS):
        want = jnp.where(dst < GDN_G_LANE, grp * GDN_HB + dst, GDN_V_HEADS + grp * GDN_HB + dst - GDN_G_LANE)
        sel = jnp.where((src == want) & ((dst % GDN_G_LANE) < GDN_HB), 1.0, 0.0).astype(BF16)
        bg_ref[grp] = _dot_sel_rhs(vals, sel)


def _gdn_prep(p_main, p_small, conv_w, a_log, dt_bias, b, s):
    n = b * s
    ts = min(GDN_TS, s)
    nt = s // ts
    neg_a = jnp.zeros((LANES,), F32).at[GDN_V_HEADS:2 * GDN_V_HEADS].set(-jnp.exp(a_log.astype(F32)))
    dt_b = jnp.zeros((LANES,), F32).at[GDN_V_HEADS:2 * GDN_V_HEADS].set(dt_bias.astype(F32))
    vecs = _stack_rows([neg_a, dt_b])
    conv8 = jnp.pad(conv_w, ((0, 8 - GDN_CONV), (0, 0)))
    row = lambda c: pl.BlockSpec((ts, c), lambda bi, i: (bi * nt + i, 0))
    full = lambda a: pl.BlockSpec(a.shape, lambda bi, i: (0, 0))
    return pl.pallas_call(
        functools.partial(_gdn_prep_kernel, ts=ts),
        grid=(b, nt),
        in_specs=[row(GDN_QKV), row(LANES), full(conv8), full(vecs)],
        out_specs=[row(GDN_KEY_WIDTH), row(GDN_KEY_WIDTH), row(GDN_VAL_WIDTH),
                   pl.BlockSpec((GDN_GROUPS, ts, LANES), lambda bi, i: (0, bi * nt + i, 0))],
        out_shape=[jax.ShapeDtypeStruct((n, GDN_KEY_WIDTH), F32), jax.ShapeDtypeStruct((n, GDN_KEY_WIDTH), F32),
                   jax.ShapeDtypeStruct((n, GDN_VAL_WIDTH), F32), jax.ShapeDtypeStruct((GDN_GROUPS, n, LANES), F32)],
        scratch_shapes=[pltpu.VMEM((8, GDN_QKV), F32)],
        compiler_params=_params("arbitrary", "arbitrary"),
        name="gdn_prep",
    )(p_main, p_small, conv8, vecs)


def _gdn_chunk_kernel(q_ref, k_ref, v_ref, z_ref, bg_ref, ng_ref, o_ref, state_ref, *, c):
    @pl.when(pl.program_id(2) == 0)
    def _():
        state_ref[...] = jnp.zeros_like(state_ref)

    lower = jnp.where(lax.broadcasted_iota(jnp.int32, (c, c), 0)
                      >= lax.broadcasted_iota(jnp.int32, (c, c), 1), 1.0, 0.0).astype(BF16)
    bg = bg_ref[0]
    cum = _dot_sel_lhs(lower, bg)
    upper = jnp.where(lax.broadcasted_iota(jnp.int32, (c, c), 0)
                      <= lax.broadcasted_iota(jnp.int32, (c, c), 1), 1.0, 0.0).astype(BF16)
    cum_t = sum(lax.dot_general(piece, upper, (((0,), (0,)), ((), ())), preferred_element_type=F32)
                for piece in _split3(bg))

    rows = GDN_STACK * c
    rr = lax.broadcasted_iota(jnp.int32, (rows, rows), 0)
    cc = lax.broadcasted_iota(jnp.int32, (rows, rows), 1)
    same = (rr // c) == (cc // c)
    strict = same & (rr > cc)
    incl = same & (rr >= cc)
    spread_cols = jnp.where(lax.broadcasted_iota(jnp.int32, (c, rows), 0)
                            == lax.broadcasted_iota(jnp.int32, (c, rows), 1) % c, 1.0, 0.0).astype(BF16)
    stack = lambda parts: jnp.concatenate(parts, axis=0)

    for grp in range(GDN_HB // GDN_STACK):
        heads = range(grp * GDN_STACK, (grp + 1) * GDN_STACK)
        qk = lambda i: slice((i // 2) * GDN_HEAD_DIM, (i // 2 + 1) * GDN_HEAD_DIM)
        vs = lambda i: slice(i * GDN_HEAD_DIM, (i + 1) * GDN_HEAD_DIM)
        q = stack([q_ref[:, qk(i)] for i in heads])
        k = stack([k_ref[:, qk(i)] for i in heads])
        v = stack([v_ref[:, vs(i)] for i in heads])
        beta = stack([bg[:, i:i + 1] for i in heads])
        g_col = stack([cum[:, GDN_G_LANE + i:GDN_G_LANE + i + 1] for i in heads])
        g_end = stack([jnp.broadcast_to(cum[c - 1:c, GDN_G_LANE + i:GDN_G_LANE + i + 1], (c, 1)) for i in heads])
        g_own = stack([jnp.broadcast_to(cum_t[GDN_G_LANE + i:GDN_G_LANE + i + 1, :], (c, c)) for i in heads])
        d_own = jnp.exp(jnp.minimum(g_col - g_own, 0.0))
        decay = jnp.where(incl, _dot(d_own, spread_cols), 0.0)
        kb = k * beta
        low = jnp.where(strict, _dot_nt(kb, k) * decay, 0.0)
        t_inv = _neumann_inverse(-low, rows, nilpotent=c)
        e_col = jnp.exp(g_col)
        u = _dot(t_inv, v * beta)
        w = _dot(t_inv, kb * e_col)
        attn = _dot_nt(q, k) * decay
        k_dec = k * jnp.exp(g_end - g_col)
        q_hat = q * e_col - _dot(attn, w)
        o_in = _dot(attn, u)
        for j, i in enumerate(heads):
            rs = slice(j * c, (j + 1) * c)
            ktw = _dot_tn(k_dec[rs], w[rs])
            nn = _dot_tn(k_dec[rs], u[rs])
            st = state_ref[i]
            o = _dot(q_hat[rs], st) + o_in[rs]
            state_ref[i] = jnp.exp(g_end[rs][0:1, :]) * st + (nn - _dot(ktw, st))
            o = o * lax.rsqrt(jnp.mean(o * o, axis=-1, keepdims=True) + GDN_NORM_EPS) * ng_ref[0:1, :]
            o_ref[:, vs(i)] = (o * _silu(z_ref[:, vs(i)])).astype(o_ref.dtype)


def _gdn_chunks(q, k, v, p_main, bg, norm_g, b, s):
    c = min(GDN_CHUNK, s)
    nc = s // c
    kcols = GDN_HB // 2 * GDN_HEAD_DIM
    vcols = GDN_HB * GDN_HEAD_DIM
    z_blk0 = GDN_QKV // vcols
    ng = _stack_rows([norm_g])
    return pl.pallas_call(
        functools.partial(_gdn_chunk_kernel, c=c),
        grid=(b, GDN_GROUPS, nc),
        in_specs=[pl.BlockSpec((c, kcols), lambda bi, g, i: (bi * nc + i, g)),
                  pl.BlockSpec((c, kcols), lambda bi, g, i: (bi * nc + i, g)),
                  pl.BlockSpec((c, vcols), lambda bi, g, i: (bi * nc + i, g)),
                  pl.BlockSpec((c, vcols), lambda bi, g, i: (bi * nc + i, z_blk0 + g)),
                  pl.BlockSpec((1, c, LANES), lambda bi, g, i: (g, bi * nc + i, 0)),
                  pl.BlockSpec(ng.shape, lambda bi, g, i: (0, 0))],
        out_specs=pl.BlockSpec((c, vcols), lambda bi, g, i: (bi * nc + i, g)),
        out_shape=jax.ShapeDtypeStruct((b * s, GDN_VAL_WIDTH), BF16),
        scratch_shapes=[pltpu.VMEM((GDN_HB, GDN_HEAD_DIM, GDN_HEAD_DIM), F32)],
        compiler_params=_params("arbitrary", "arbitrary", "arbitrary"),
        name="gdn_chunks",
    )(q, k, v, p_main, bg, ng)


def _odd_mixer(h_bf, b, s, w_in, conv_w, a_log, dt_bias, norm_g, w_out):
    n_main = GDN_QKV + GDN_VAL_WIDTH
    p_main = _matmul(h_bf, w_in, F32, m=n_main)
    w_small = jnp.pad(w_in[:, n_main:], ((0, 0), (0, LANES - 2 * GDN_V_HEADS)))
    p_small = _matmul(h_bf, w_small, F32)
    q, k, v, bg = _gdn_prep(p_main, p_small, conv_w, a_log, dt_bias, b, s)
    o = _gdn_chunks(q, k, v, p_main, bg, norm_g, b, s)
    return _matmul(o, w_out, F32, tn=512)


ROUTER_TS = 512
GROUP_SIZE = N_EXPERTS // N_GROUPS


def _beats(other, mine, other_idx, my_idx):
    return jnp.where((other > mine) | ((other == mine) & (other_idx < my_idx)), 1, 0)


def _router_kernel(h_ref, rt_ref, bias_ref, eid_ref, pos_ref, w_ref, cnt_ref, run_ref, *, ts):
    @pl.when(pl.program_id(0) == 0)
    def _():
        run_ref[...] = jnp.zeros_like(run_ref)

    logits = lax.dot_general(rt_ref[...], h_ref[...], (((1,), (1,)), ((), ())),
                             precision=HI, preferred_element_type=F32)
    scores = _sigmoid(logits)
    choice = scores + jnp.tile(bias_ref[...], (1, ts // LANES))
    neg_inf = jnp.float32(-jnp.inf)

    grouped = choice.reshape(N_GROUPS, GROUP_SIZE, ts)
    m1 = jnp.max(grouped, axis=1, keepdims=True)
    ties = jnp.sum(jnp.where(grouped == m1, 1, 0), axis=1, keepdims=True)
    m2 = jnp.max(jnp.where(grouped < m1, grouped, neg_inf), axis=1, keepdims=True)
    group_score = (m1 + jnp.where(ties >= 2, m1, m2)).reshape(N_GROUPS, ts)

    g_idx = lax.broadcasted_iota(jnp.int32, (N_GROUPS, ts), 0)
    g_rank = jnp.zeros((N_GROUPS, ts), jnp.int32)
    for g in range(N_GROUPS):
        g_rank = g_rank + _beats(group_score[g:g + 1, :], group_score, g, g_idx)
    g_keep = jnp.where(g_rank < TOPK_GROUPS, 1.0, 0.0).reshape(N_GROUPS, 1, ts)
    e_keep = jnp.broadcast_to(g_keep, (N_GROUPS, GROUP_SIZE, ts)).reshape(N_EXPERTS, ts)
    masked = jnp.where(e_keep > 0.5, choice, neg_inf)

    e_idx = lax.broadcasted_iota(jnp.int32, (N_EXPERTS, ts), 0)
    e_rank = jnp.zeros((N_EXPERTS, ts), jnp.int32)
    for e in range(N_EXPERTS):
        e_rank = e_rank + _beats(masked[e:e + 1, :], masked, e, e_idx)
    chosen = e_rank < TOP_K
    top_w = jnp.where(chosen, scores, 0.0)
    gates = top_w / jnp.sum(top_w, axis=0, keepdims=True) * ROUTED_SCALE

    picks = jnp.where(chosen, 1.0, 0.0).astype(BF16)
    earlier = jnp.where(lax.broadcasted_iota(jnp.int32, (ts, ts), 0)
                        < lax.broadcasted_iota(jnp.int32, (ts, ts), 1), 1.0, 0.0).astype(BF16)
    pos = jnp.dot(picks, earlier, preferred_element_type=F32) + jnp.tile(run_ref[...], (1, ts // LANES))
    run_ref[...] += jnp.dot(picks, jnp.ones((ts, LANES), BF16), preferred_element_type=F32)
    cnt_ref[...] = run_ref[...]

    e_f = e_idx.astype(F32)
    ids, poss, ws = [], [], []
    for k in range(TOP_K):
        hit = e_rank == k
        ids.append(jnp.sum(jnp.where(hit, e_f, 0.0), axis=0, keepdims=True))
        poss.append(jnp.sum(jnp.where(hit, pos, 0.0), axis=0, keepdims=True))
        ws.append(jnp.sum(jnp.where(hit, gates, 0.0), axis=0, keepdims=True))
    eid_ref[...] = jnp.concatenate(ids, axis=0).astype(jnp.int32)
    pos_ref[...] = jnp.concatenate(poss, axis=0).astype(jnp.int32)
    w_ref[...] = jnp.concatenate(ws, axis=0)


def _router(h, router, router_bias):
    n = h.shape[0]
    ts = min(ROUTER_TS, n)
    rt = router.T.astype(F32)
    bias = jnp.broadcast_to(router_bias.astype(F32)[:, None], (N_EXPERTS, LANES))
    per_tok = pl.BlockSpec((TOP_K, ts), lambda i: (0, i))
    eid, pos, w, cnt = pl.pallas_call(
        functools.partial(_router_kernel, ts=ts),
        grid=(n // ts,),
        in_specs=[pl.BlockSpec((ts, D_MODEL), lambda i: (i, 0)),
                  pl.BlockSpec(rt.shape, lambda i: (0, 0)),
                  pl.BlockSpec(bias.shape, lambda i: (0, 0))],
        out_specs=[per_tok, per_tok, per_tok, pl.BlockSpec((N_EXPERTS, LANES), lambda i: (0, 0))],
        out_shape=[jax.ShapeDtypeStruct((TOP_K, n), jnp.int32), jax.ShapeDtypeStruct((TOP_K, n), jnp.int32),
                   jax.ShapeDtypeStruct((TOP_K, n), F32), jax.ShapeDtypeStruct((N_EXPERTS, LANES), F32)],
        scratch_shapes=[pltpu.VMEM((N_EXPERTS, LANES), F32)],
        compiler_params=_params("arbitrary"),
        name="router",
    )(h, rt, bias)
    return eid, pos, w, cnt[:, 0].astype(jnp.int32)


MOE_TM = 256
MOE_TS = 256
DMA_QUEUES = 2


def _swiglu_hidden(x, w_gu, ff):
    gu = jnp.dot(x, w_gu, preferred_element_type=F32)
    return _silu(gu[:, :ff]) * gu[:, ff:]


def _tile_copy(src, src_row, dst, dst_row, sem):
    return pltpu.make_async_copy(src.at[pl.ds(pl.multiple_of(src_row, PACK_ROWS), PACK_ROWS), :],
                                 dst.at[pl.ds(pl.multiple_of(dst_row, PACK_ROWS), PACK_ROWS), :], sem)


def _dispatch_kernel(slots_ref, hp_ref, xs_in_ref, xs_ref, sem, *, ts):
    del xs_in_ref

    def issue(t, carry):
        for k in range(TOP_K):
            slot = slots_ref[0, 0, t * TOP_K + k]
            _tile_copy(hp_ref, t * PACK_ROWS, xs_ref, slot * PACK_ROWS, sem).start(priority=k % DMA_QUEUES)
        return carry

    lax.fori_loop(0, ts, issue, 0)
    for _ in range(TOP_K):
        pltpu.make_async_copy(hp_ref, xs_ref.at[pl.ds(0, ts * PACK_ROWS), :], sem).wait()


def _dispatch(hp, slots, n_slots):
    n = hp.shape[0] // PACK_ROWS
    ts = min(MOE_TS, n)
    xs0 = jnp.zeros((n_slots * PACK_ROWS, LANES), jnp.uint32)
    return pl.pallas_call(
        functools.partial(_dispatch_kernel, ts=ts),
        grid=(n // ts,),
        in_specs=[pl.BlockSpec((1, 1, ts * TOP_K), lambda i: (i, 0, 0), memory_space=pltpu.SMEM),
                  pl.BlockSpec((ts * PACK_ROWS, LANES), lambda i: (i, 0)),
                  pl.BlockSpec(memory_space=pl.ANY)],
        out_specs=pl.BlockSpec(memory_space=pl.ANY),
        out_shape=jax.ShapeDtypeStruct(xs0.shape, jnp.uint32),
        scratch_shapes=[pltpu.SemaphoreType.DMA(())],
        input_output_aliases={2: 0},
        compiler_params=_params("arbitrary"),
        name="moe_dispatch",
    )(slots, hp, xs0)


def _experts_kernel(te_ref, nv_ref, xs_ref, wg_ref, wu_ref, wd_ref, ys_ref, gu_ref, dn_ref):
    t = pl.program_id(0)
    live = t < nv_ref[0]
    new_expert = (t == 0) | (te_ref[t] != te_ref[jnp.maximum(t - 1, 0)])

    @pl.when(live & new_expert)
    def _():
        gu_ref[:, :EXPERT_FF] = wg_ref[0].astype(BF16)
        gu_ref[:, EXPERT_FF:] = wu_ref[0].astype(BF16)
        dn_ref[...] = wd_ref[0].astype(BF16)

    @pl.when(live)
    def _():
        x = _load_packed(xs_ref, 0, MOE_TM)
        hid = _swiglu_hidden(x, gu_ref[...], EXPERT_FF)
        y = jnp.dot(hid.astype(BF16), dn_ref[...], preferred_element_type=F32)
        _store_packed(ys_ref, y, MOE_TM)

    @pl.when(jnp.logical_not(live))
    def _():
        ys_ref[...] = jnp.zeros_like(ys_ref)


def _experts(xs, tile_expert, n_valid, e_gate, e_up, e_down):
    n_tiles = tile_expert.shape[0]
    rows = MOE_TM * PACK_ROWS
    weight = lambda w: pl.BlockSpec((1,) + w.shape[1:], lambda t, te, nv: (te[t], 0, 0))
    return pl.pallas_call(
        _experts_kernel,
        grid_spec=pltpu.PrefetchScalarGridSpec(
            num_scalar_prefetch=2,
            grid=(n_tiles,),
            in_specs=[pl.BlockSpec((rows, LANES), lambda t, te, nv: (t, 0)),
                      weight(e_gate), weight(e_up), weight(e_down)],
            out_specs=pl.BlockSpec((rows, LANES), lambda t, te, nv: (t, 0)),
            scratch_shapes=[pltpu.VMEM((D_MODEL, 2 * EXPERT_FF), BF16), pltpu.VMEM((EXPERT_FF, D_MODEL), BF16)]),
        out_shape=jax.ShapeDtypeStruct(xs.shape, jnp.uint32),
        compiler_params=_params("arbitrary"),
        name="moe_experts",
    )(tile_expert, n_valid, xs, e_gate, e_up, e_down)


def _combine_kernel(slots_ref, w_ref, hb_ref, h_ref, ys_ref, sgu_ref, sdn_ref, g_ref, b_ref,
                    o_ref, ob_ref, ybuf, sem, *, ts):
    def issue(t, carry):
        for k in range(TOP_K):
            slot = slots_ref[0, 0, t * TOP_K + k]
            _tile_copy(ys_ref, slot * PACK_ROWS, ybuf, (k * ts + t) * PACK_ROWS, sem).start(priority=k % DMA_QUEUES)
        return carry

    lax.fori_loop(0, ts, issue, 0)
    hid = _swiglu_hidden(hb_ref[...], sgu_ref[...], SHARED_FF)
    y = jnp.dot(hid.astype(BF16), sdn_ref[...], preferred_element_type=F32)
    for _ in range(TOP_K):
        pltpu.make_async_copy(ys_ref.at[pl.ds(0, ts * PACK_ROWS), :], ybuf.at[pl.ds(0, ts * PACK_ROWS), :], sem).wait()
    for k in range(TOP_K):
        y = y + w_ref[:, k:k + 1] * _load_packed(ybuf, k * ts * PACK_ROWS, ts).astype(F32)
    out = _ln_rows(DN_ALPHA * h_ref[...] + y, g_ref[...], b_ref[...])
    o_ref[...] = out
    ob_ref[...] = out.astype(BF16)


def _combine(h, h_bf, ys, slots, w_tok, s_gu, s_dn, ln_g, ln_b):
    n, d = h.shape
    ts = min(MOE_TS, n)
    row = lambda c: pl.BlockSpec((ts, c), lambda i: (i, 0))
    full = lambda a: pl.BlockSpec(a.shape, lambda i: (0,) * a.ndim)
    g2, b2 = ln_g.reshape(1, d), ln_b.reshape(1, d)
    return pl.pallas_call(
        functools.partial(_combine_kernel, ts=ts),
        grid=(n // ts,),
        in_specs=[pl.BlockSpec((1, 1, ts * TOP_K), lambda i: (i, 0, 0), memory_space=pltpu.SMEM),
                  row(TOP_K), row(d), row(d), pl.BlockSpec(memory_space=pl.ANY),
                  full(s_gu), full(s_dn), full(g2), full(b2)],
        out_specs=[row(d), row(d)],
        out_shape=[jax.ShapeDtypeStruct((n, d), F32), jax.ShapeDtypeStruct((n, d), BF16)],
        scratch_shapes=[pltpu.VMEM((TOP_K * ts * PACK_ROWS, LANES), jnp.uint32), pltpu.SemaphoreType.DMA(())],
        compiler_params=_params("arbitrary"),
        name="moe_combine",
    )(slots, w_tok, h_bf, h, ys, s_gu, s_dn, g2, b2)


def _ffn(h, h_bf, hp, router, router_bias, e_gate, e_up, e_down, s_gate, s_up, s_down, ln_g, ln_b):
    n = h.shape[0]
    eid, pos, w, counts = _router(h, router, router_bias)
    padded = (counts + MOE_TM - 1) // MOE_TM * MOE_TM
    ends = jnp.cumsum(padded)
    starts = ends - padded
    n_slots = n * TOP_K + N_EXPERTS * MOE_TM
    n_tiles = n_slots // MOE_TM
    ts = min(MOE_TS, n)
    experts = jnp.arange(N_EXPERTS, dtype=jnp.int32)
    start_of = jnp.sum(jnp.where(eid[..., None] == experts, starts, 0), axis=-1)
    slots = (start_of + pos).T.reshape(n // ts, 1, ts * TOP_K)
    tile_start = jnp.arange(n_tiles, dtype=jnp.int32) * MOE_TM
    tile_expert = jnp.minimum(jnp.sum((ends[None, :] <= tile_start[:, None]).astype(jnp.int32), axis=1),
                              N_EXPERTS - 1)
    n_valid = (ends[-1:] // MOE_TM).astype(jnp.int32)

    s_gu = jnp.concatenate([s_gate, s_up], axis=1).astype(BF16)
    xs = _dispatch(hp, slots, n_slots)
    ys = _experts(xs, tile_expert, n_valid, e_gate, e_up, e_down)
    return _combine(h, h_bf, ys, slots, w.T, s_gu, s_down.astype(BF16), ln_g, ln_b)


def _trunk(x, layers):
    b, s, d = x.shape
    n = b * s
    x = x.reshape(n, d)
    x_bf = x.astype(BF16)
    v_first = None
    for i, (mixer, ffn) in enumerate(layers):
        if i % 2 == 0:
            mix, v_first = _even_mixer(x_bf, v_first, b, s, *mixer)
        else:
            mix = _odd_mixer(x_bf, b, s, *mixer)
        h, h_bf, hp = _add_ln(x, mix, ffn[0], ffn[1])
        x, x_bf = _ffn(h, h_bf, hp, *ffn[2:])
    return x.reshape(b, s, d)


def kernel(x, l0_w_in, l0_mu, l0_w_up, l0_w0, l0_a_up, l0_a0, l0_g_up, l0_k_k, l0_k_a, l0_r_k, l0_lnx_g, l0_lnx_b, l0_w_out, l0_ln1_g, l0_ln1_b, l0_router, l0_router_bias, l0_e_gate, l0_e_up, l0_e_down, l0_s_gate, l0_s_up, l0_s_down, l0_ln2_g, l0_ln2_b, l1_w_in, l1_conv, l1_a_log, l1_dt_bias, l1_norm_g, l1_w_out, l1_ln1_g, l1_ln1_b, l1_router, l1_router_bias, l1_e_gate, l1_e_up, l1_e_down, l1_s_gate, l1_s_up, l1_s_down, l1_ln2_g, l1_ln2_b, l2_w_in, l2_mu, l2_w_up, l2_w0, l2_a_up, l2_a0, l2_v_up, l2_v0, l2_g_up, l2_k_k, l2_k_a, l2_r_k, l2_lnx_g, l2_lnx_b, l2_w_out, l2_ln1_g, l2_ln1_b, l2_router, l2_router_bias, l2_e_gate, l2_e_up, l2_e_down, l2_s_gate, l2_s_up, l2_s_down, l2_ln2_g, l2_ln2_b, l3_w_in, l3_conv, l3_a_log, l3_dt_bias, l3_norm_g, l3_w_out, l3_ln1_g, l3_ln1_b, l3_router, l3_router_bias, l3_e_gate, l3_e_up, l3_e_down, l3_s_gate, l3_s_up, l3_s_down, l3_ln2_g, l3_ln2_b):
    layers = [
        ((l0_w_in, l0_mu, l0_w_up, l0_w0, l0_a_up, l0_a0, None, None, l0_g_up, l0_k_k, l0_k_a, l0_r_k,
          l0_lnx_g, l0_lnx_b, l0_w_out),
         (l0_ln1_g, l0_ln1_b, l0_router, l0_router_bias, l0_e_gate, l0_e_up, l0_e_down,
          l0_s_gate, l0_s_up, l0_s_down, l0_ln2_g, l0_ln2_b)),
        ((l1_w_in, l1_conv, l1_a_log, l1_dt_bias, l1_norm_g, l1_w_out),
         (l1_ln1_g, l1_ln1_b, l1_router, l1_router_bias, l1_e_gate, l1_e_up, l1_e_down,
          l1_s_gate, l1_s_up, l1_s_down, l1_ln2_g, l1_ln2_b)),
        ((l2_w_in, l2_mu, l2_w_up, l2_w0, l2_a_up, l2_a0, l2_v_up, l2_v0, l2_g_up, l2_k_k, l2_k_a, l2_r_k,
          l2_lnx_g, l2_lnx_b, l2_w_out),
         (l2_ln1_g, l2_ln1_b, l2_router, l2_router_bias, l2_e_gate, l2_e_up, l2_e_down,
          l2_s_gate, l2_s_up, l2_s_down, l2_ln2_g, l2_ln2_b)),
        ((l3_w_in, l3_conv, l3_a_log, l3_dt_bias, l3_norm_g, l3_w_out),
         (l3_ln1_g, l3_ln1_b, l3_router, l3_router_bias, l3_e_gate, l3_e_up, l3_e_down,
          l3_s_gate, l3_s_up, l3_s_down, l3_ln2_g, l3_ln2_b)),
    ]
    return _trunk(x, layers)
```

```python
import functools
import math

import jax
import jax.numpy as jnp
from jax import lax
from jax.experimental import pallas as pl
from jax.experimental.pallas import tpu as pltpu

D_MODEL = 2048
DEPTH = 4
SB_HEADS = 16
SB_HEAD_DIM = 64
SB_WIDTH = SB_HEADS * SB_HEAD_DIM
RW_HEADS = 16
RW_HEAD_DIM = 64
RW_WIDTH = RW_HEADS * RW_HEAD_DIM
RW_DECAY_RANK = 96
RW_ICLR_RANK = 96
RW_VRES_RANK = 64
RW_GATE_RANK = 256
RW_GN_EPS = 64e-5
GDN_QK_HEADS = 16
GDN_V_HEADS = 32
GDN_HEAD_DIM = 128
GDN_KEY_WIDTH = GDN_QK_HEADS * GDN_HEAD_DIM
GDN_VAL_WIDTH = GDN_V_HEADS * GDN_HEAD_DIM
GDN_CONV = 4
GDN_NORM_EPS = 1e-6
N_EXPERTS = 64
TOP_K = 8
N_GROUPS = 8
TOPK_GROUPS = 4
EXPERT_FF = 384
SHARED_FF = 384
ROUTED_SCALE = 2.5
DN_ALPHA = (2 * DEPTH) ** 0.25
LN_EPS = 1e-5
L2_EPS = 1e-6

LANES = 128
VMEM_LIMIT = 56 * 1024 * 1024
F32 = jnp.float32
BF16 = jnp.bfloat16
HI = lax.Precision.HIGHEST


def _params(*sem):
    return pltpu.CompilerParams(dimension_semantics=sem, vmem_limit_bytes=VMEM_LIMIT)


def _contract(a, b, dims):
    return lax.dot_general(a.astype(BF16), b.astype(BF16), (dims, ((), ())), preferred_element_type=F32)


def _dot(a, b):
    return _contract(a, b, ((1,), (0,)))


def _dot_nt(a, b):
    return _contract(a, b, ((1,), (1,)))


def _dot_tn(a, b):
    return _contract(a, b, ((0,), (0,)))


def _sigmoid(x):
    return 1.0 / (1.0 + jnp.exp(-x))


def _softplus(x):
    return jnp.maximum(x, 0.0) + jnp.log(1.0 + jnp.exp(-jnp.abs(x)))


def _silu(x):
    return x * _sigmoid(x)


def _matmul_kernel(x_ref, w_ref, o_ref, wb_ref):
    @pl.when(pl.program_id(1) == 0)
    def _():
        wb_ref[...] = w_ref[...].astype(BF16)

    o_ref[...] = jnp.dot(x_ref[...], wb_ref[...], preferred_element_type=F32).astype(o_ref.dtype)


def _matmul(x, w, out_dtype, tm=1024, tn=1024, m=None):
    n, k = x.shape
    m = w.shape[1] if m is None else m
    tm = min(tm, n)
    tn = min(tn, m)
    assert n % tm == 0 and m % tn == 0, (n, m, tm, tn)
    return pl.pallas_call(
        _matmul_kernel,
        grid=(m // tn, n // tm),
        in_specs=[pl.BlockSpec((tm, k), lambda j, i: (i, 0)),
                  pl.BlockSpec((k, tn), lambda j, i: (0, j))],
        out_specs=pl.BlockSpec((tm, tn), lambda j, i: (i, j)),
        out_shape=jax.ShapeDtypeStruct((n, m), out_dtype),
        scratch_shapes=[pltpu.VMEM((k, tn), BF16)],
        compiler_params=_params("arbitrary", "arbitrary"),
        name="matmul",
    )(x, w)


def _ln_rows(y, g, b):
    mean = jnp.mean(y, axis=-1, keepdims=True)
    yc = y - mean
    var = jnp.mean(yc * yc, axis=-1, keepdims=True)
    return yc * lax.rsqrt(var + LN_EPS) * g + b


PACK_ROWS = D_MODEL // 2 // LANES


def _bf16_bits(x):
    return pltpu.bitcast(x.astype(BF16).astype(F32), jnp.uint32)


def _store_packed(ref, x, rows):
    half = D_MODEL // 2
    words = (_bf16_bits(x[:, half:]) & jnp.uint32(0xFFFF0000)) | (_bf16_bits(x[:, :half]) >> 16)
    for s in range(PACK_ROWS):
        ref[pl.ds(s, rows, stride=PACK_ROWS), :] = words[:, s * LANES:(s + 1) * LANES]


def _load_packed(ref, start, rows):
    lo, hi = [], []
    for s in range(PACK_ROWS):
        words = ref[pl.ds(start + s, rows, stride=PACK_ROWS), :]
        lo.append(pltpu.bitcast(words << 16, F32).astype(BF16))
        hi.append(pltpu.bitcast(words & jnp.uint32(0xFFFF0000), F32).astype(BF16))
    return jnp.concatenate(lo + hi, axis=1)


def _add_ln_kernel(x_ref, m_ref, g_ref, b_ref, o_ref, ob_ref, op_ref, *, tm):
    y = DN_ALPHA * x_ref[...] + m_ref[...].astype(F32)
    out = _ln_rows(y, g_ref[...], b_ref[...])
    o_ref[...] = out
    ob_ref[...] = out.astype(BF16)
    _store_packed(op_ref, out, tm)


def _add_ln(x, mix, g, b, tm=512):
    n, d = x.shape
    tm = min(tm, n)
    row = pl.BlockSpec((tm, d), lambda i: (i, 0))
    vec = pl.BlockSpec((1, d), lambda i: (0, 0))
    return pl.pallas_call(
        functools.partial(_add_ln_kernel, tm=tm),
        grid=(n // tm,),
        in_specs=[row, row, vec, vec],
        out_specs=[row, row, pl.BlockSpec((tm * PACK_ROWS, LANES), lambda i: (i, 0))],
        out_shape=[jax.ShapeDtypeStruct((n, d), F32), jax.ShapeDtypeStruct((n, d), BF16),
                   jax.ShapeDtypeStruct((n * PACK_ROWS, LANES), jnp.uint32)],
        compiler_params=_params("arbitrary"),
        name="add_ln",
    )(x, mix, g.reshape(1, d), b.reshape(1, d))


SB_TQ = 256
SB_TK = 128
SB_UNDERFLOW = -104.0


def _sb_kernel(q_ref, k_ref, v_ref, o_ref, *, tq, tk):
    i = pl.program_id(2)
    q = q_ref[0] * jnp.asarray(SB_HEAD_DIM ** -0.5, BF16)
    lane = lax.broadcasted_iota(jnp.int32, (tq, LANES), 1)
    zero = jnp.zeros_like(q)
    q_heads = (jnp.where(lane < SB_HEAD_DIM, q, zero), jnp.where(lane >= SB_HEAD_DIM, q, zero))
    q_pos = i * tq + lax.broadcasted_iota(jnp.int32, (tq, tk), 0)
    k_off = lax.broadcasted_iota(jnp.int32, (tq, tk), 1)
    r2 = lax.broadcasted_iota(jnp.int32, (2 * tk, tk + LANES), 0) % tk
    c2 = lax.broadcasted_iota(jnp.int32, (2 * tk, tk + LANES), 1)
    cum = jnp.where((r2 > c2) | (c2 >= tk), 1.0, 0.0).astype(BF16)
    n_diag = tq // tk
    first_diag = i * n_diag

    def block(j, carry, on_diagonal):
        start = pl.multiple_of(j * tk, tk)
        ks = k_ref[0, pl.ds(start, tk), :]
        vs = v_ref[0, pl.ds(start, tk), :]
        causal = (start + k_off) < q_pos
        new = []
        for h in range(2):
            acc, later_blocks = carry[2 * h], carry[2 * h + 1]
            z = _dot_nt(q_heads[h], ks)
            sp = _softplus(z)
            log_keep = jnp.where(causal, -sp, 0.0) if on_diagonal else -sp
            hi = log_keep.astype(BF16)
            lo = (log_keep - hi.astype(F32)).astype(BF16)
            sums = jnp.dot(jnp.concatenate([hi, lo], axis=1), cum, preferred_element_type=F32)
            later = sums[:, :tk] + jnp.tile(later_blocks, (1, tk // LANES))
            w = jnp.exp(z - sp + later)
            if on_diagonal:
                w = jnp.where(causal, w, 0.0)
            acc = acc + jnp.dot(w.astype(BF16), vs, preferred_element_type=F32)
            new += [acc, later_blocks + sums[:, tk:]]
        return tuple(new)

    carry = tuple(jnp.zeros((tq, LANES), F32) for _ in range(4))
    for d in range(n_diag):
        carry = block(first_diag + n_diag - 1 - d, carry, True)

    def live(carry):
        return jnp.max(jnp.maximum(carry[1], carry[3])) > SB_UNDERFLOW

    def cond(state):
        return (state[0] >= 0) & state[1]

    def body(state):
        carry = block(state[0], state[2], False)
        return state[0] - 1, live(carry), carry

    _, _, res = lax.while_loop(cond, body, (first_diag - 1, live(carry), carry))
    o_ref[0] = jnp.where(lane < SB_HEAD_DIM, res[0], res[2]).astype(o_ref.dtype)


def _stick_breaking(qkv, b, s):
    tq = min(SB_TQ, s)
    tk = min(SB_TK, s)
    pairs = SB_WIDTH // LANES
    return pl.pallas_call(
        functools.partial(_sb_kernel, tq=tq, tk=tk),
        grid=(b, pairs, s // tq),
        in_specs=[pl.BlockSpec((1, tq, LANES), lambda bi, p, i: (bi, i, p)),
                  pl.BlockSpec((1, s, LANES), lambda bi, p, i: (bi, 0, pairs + p)),
                  pl.BlockSpec((1, s, LANES), lambda bi, p, i: (bi, 0, 2 * pairs + p))],
        out_specs=pl.BlockSpec((1, tq, LANES), lambda bi, p, i: (bi, i, p)),
        out_shape=jax.ShapeDtypeStruct((b, s, SB_WIDTH), BF16),
        compiler_params=_params("arbitrary", "arbitrary", "arbitrary"),
        name="stick_breaking",
    )(qkv, qkv, qkv)


def _split3(x):
    hi = x.astype(BF16)
    r1 = x - hi.astype(F32)
    mid = r1.astype(BF16)
    lo = (r1 - mid.astype(F32)).astype(BF16)
    return hi, mid, lo


def _dot_sel_rhs(x, sel):
    return sum(jnp.dot(p, sel, preferred_element_type=F32) for p in _split3(x))


def _dot_sel_lhs(sel, x):
    return sum(jnp.dot(sel, p, preferred_element_type=F32) for p in _split3(x))


def _head_sum_matrices(width, head_dim):
    heads = width // head_dim
    assert heads <= LANES
    c = lax.broadcasted_iota(jnp.int32, (width, LANES), 0) // head_dim
    h = lax.broadcasted_iota(jnp.int32, (width, LANES), 1)
    gather = jnp.where(c == h, 1.0, 0.0).astype(BF16)
    ct = lax.broadcasted_iota(jnp.int32, (LANES, width), 1) // head_dim
    ht = lax.broadcasted_iota(jnp.int32, (LANES, width), 0)
    spread = jnp.where(ct == ht, 1.0, 0.0).astype(BF16)
    return gather, spread


def _per_head_sum(x, gather, spread):
    return _dot_sel_rhs(_dot_sel_rhs(x, gather), spread)


def _neumann_inverse(a, size, nilpotent):
    blocks = size // nilpotent
    same = (lax.broadcasted_iota(jnp.int32, (size, size), 0) // nilpotent
            == lax.broadcasted_iota(jnp.int32, (size, size), 1) // nilpotent)
    expand = lambda x: jnp.where(same, jnp.concatenate([x] * blocks, axis=0), 0.0)
    eye = jnp.where(lax.broadcasted_iota(jnp.int32, (nilpotent, size), 0)
                    == lax.broadcasted_iota(jnp.int32, (nilpotent, size), 1) % nilpotent, 1.0, 0.0)
    power = sum(a[i * nilpotent:(i + 1) * nilpotent] for i in range(blocks))
    inv = eye + power
    span = 2
    while span < nilpotent:
        power = _dot(power, expand(power))
        inv = inv + _dot(inv, expand(power))
        span *= 2
    return expand(inv)


RW_LORA = 512
RW_COLS = 3 * RW_WIDTH + RW_LORA
RW_TS = 256
RW_CHUNK = 64
RW_STACK = 2


def _rwkv_prep_kernel(pb_ref, mu_ref, wup_ref, aup_ref, gup_ref, vup_ref, vec_ref, vfirst_ref,
                      r_ref, k_ref, v_ref, kk_ref, a_ref, lw_ref, g_ref, bonus_ref,
                      last_ref, *, ts, has_vres):
    @pl.when(pl.program_id(1) == 0)
    def _():
        last_ref[...] = jnp.zeros_like(last_ref)

    x = pb_ref[...]
    prev = pltpu.roll(x, 1, 0)
    first_row = lax.broadcasted_iota(jnp.int32, x.shape, 0) == 0
    prev = jnp.where(first_row, jnp.broadcast_to(last_ref[0:1, :], x.shape), prev)
    last_ref[0:1, :] = x[ts - 1:ts, :]
    x = x + (prev - x) * mu_ref[...]

    w = RW_WIDTH
    r, k, v, lora = x[:, :w], x[:, w:2 * w], x[:, 2 * w:3 * w], x[:, 3 * w:]
    w0, a0, v0, k_k, k_a, r_k = (vec_ref[i:i + 1, :] for i in range(6))
    w_log = -_softplus(-(w0 + _dot(jnp.tanh(lora), wup_ref[...]))) - 0.5
    lw_ref[...] = -jnp.exp(w_log)
    if has_vres:
        v = v + (vfirst_ref[...] - v) * _sigmoid(v0 + _dot(lora, vup_ref[...]))
    a = _sigmoid(a0 + _dot(lora, aup_ref[...]))
    g_ref[...] = _dot(_sigmoid(lora), gup_ref[...])
    gather, spread = _head_sum_matrices(w, RW_HEAD_DIM)
    kk = k * k_k
    kk = kk * lax.rsqrt(_per_head_sum(kk * kk, gather, spread) + L2_EPS)
    k = k * (1.0 + (a - 1.0) * k_a)
    bonus_ref[...] = _per_head_sum(r * k * r_k, gather, spread) * v
    r_ref[...] = r
    k_ref[...] = k
    v_ref[...] = v
    kk_ref[...] = kk
    a_ref[...] = a


def _rwkv_prep(pb, b, s, mu, wup, aup, gup, vup, vecs, v_first):
    n = b * s
    ts = min(RW_TS, s)
    nt = s // ts
    has_vres = v_first is not None
    if not has_vres:
        v_first = jnp.zeros((8, RW_WIDTH), F32)
        vf_spec = pl.BlockSpec((8, RW_WIDTH), lambda bi, i: (0, 0))
    else:
        vf_spec = pl.BlockSpec((ts, RW_WIDTH), lambda bi, i: (bi * nt + i, 0))
    row = lambda c: pl.BlockSpec((ts, c), lambda bi, i: (bi * nt + i, 0))
    full = lambda a: pl.BlockSpec(a.shape, lambda bi, i: (0, 0))
    outs = [jax.ShapeDtypeStruct((n, RW_WIDTH), F32)] * 8
    return pl.pallas_call(
        functools.partial(_rwkv_prep_kernel, ts=ts, has_vres=has_vres),
        grid=(b, nt),
        in_specs=[row(RW_COLS), full(mu), full(wup), full(aup), full(gup), full(vup), full(vecs), vf_spec],
        out_specs=[row(RW_WIDTH)] * 8,
        out_shape=outs,
        scratch_shapes=[pltpu.VMEM((8, RW_COLS), F32)],
        compiler_params=_params("arbitrary", "arbitrary"),
        name="rwkv_prep",
    )(pb, mu, wup, aup, gup, vup, vecs, v_first)


def _rwkv_chunk_kernel(r_ref, k_ref, v_ref, kk_ref, a_ref, lw_ref, g_ref, bonus_ref, lnx_ref,
                       o_ref, state_ref, *, c):
    @pl.when(pl.program_id(1) == 0)
    def _():
        state_ref[...] = jnp.zeros_like(state_ref)

    tri = jnp.where(lax.broadcasted_iota(jnp.int32, (c, c), 0)
                    >= lax.broadcasted_iota(jnp.int32, (c, c), 1), 1.0, 0.0).astype(BF16)

    lw = lw_ref[...]
    cum = _dot_sel_lhs(tri, lw)
    total = cum[c - 1:c, :]
    kk = kk_ref[...]
    b_vec = kk * a_ref[...]
    dec_out = jnp.exp(total - cum)
    inv_cum = jnp.exp(-cum)
    r_in = r_ref[...] * jnp.exp(cum)
    a_in = -kk * jnp.exp(cum - lw)
    b_out = b_vec * inv_cum
    k_out = k_ref[...] * inv_cum
    b_end = b_vec * dec_out
    k_end = k_ref[...] * dec_out
    v_all = v_ref[...]
    ones = jnp.ones((c, LANES), BF16)
    p_end = jnp.exp(sum(lax.dot_general(piece, ones, (((0,), (0,)), ((), ())), preferred_element_type=F32)
                        for piece in _split3(lw)))

    lane = lax.broadcasted_iota(jnp.int32, (c, LANES), 1)
    first = lane < RW_HEAD_DIM
    blk_r = lax.broadcasted_iota(jnp.int32, (LANES, LANES), 0)
    blk_c = lax.broadcasted_iota(jnp.int32, (LANES, LANES), 1)
    same_head = (blk_r // RW_HEAD_DIM) == (blk_c // RW_HEAD_DIM)

    rows = 2 * RW_STACK * c
    srr = lax.broadcasted_iota(jnp.int32, (rows, rows), 0)
    scc = lax.broadcasted_iota(jnp.int32, (rows, rows), 1)
    same_blk = (srr // c) == (scc // c)
    strict = same_blk & (srr > scc)
    incl = same_blk & (srr >= scc)
    stack = lambda parts: jnp.concatenate(parts, axis=0)
    twice = lambda xs: stack([x for x in xs for _ in range(2)])
    by_head = lambda xs: stack([jnp.where(keep, x, 0.0) for x in xs for keep in (first, ~first)])
    pick = lambda x, j: jnp.where(first, x[2 * j * c:(2 * j + 1) * c], x[(2 * j + 1) * c:(2 * j + 2) * c])

    ys = []
    for grp in range(RW_WIDTH // LANES // RW_STACK):
        sls = [slice(p * LANES, (p + 1) * LANES) for p in range(grp * RW_STACK, (grp + 1) * RW_STACK)]
        ai, ri = [a_in[:, sl] for sl in sls], [r_in[:, sl] for sl in sls]
        bo, ko, v = twice([b_out[:, sl] for sl in sls]), twice([k_out[:, sl] for sl in sls]), twice([v_all[:, sl] for sl in sls])
        ai_h, ri_h = by_head(ai), by_head(ri)
        a_ab = jnp.where(strict, _dot_nt(ai_h, bo), 0.0)
        a_ak = jnp.where(strict, _dot_nt(ai_h, ko), 0.0)
        a_rb = jnp.where(incl, _dot_nt(ri_h, bo), 0.0)
        a_rk = jnp.where(incl, _dot_nt(ri_h, ko), 0.0)
        t_inv = _neumann_inverse(a_ab, rows, nilpotent=c)
        w_rows = _dot(t_inv, twice(ai))
        u_rows = _dot(t_inv, _dot(a_ak, v))
        w_t = [pick(w_rows, j) for j in range(RW_STACK)]
        u0 = [pick(u_rows, j) for j in range(RW_STACK)]
        q_rows = _dot(a_rb, twice(w_t))
        y_rows = _dot(a_rb, twice(u0)) + _dot(a_rk, v)
        for j, sl in enumerate(sls):
            p = grp * RW_STACK + j
            be, ke, vp = b_end[:, sl], k_end[:, sl], v_all[:, sl]
            m_t = jnp.where(same_head, _dot_tn(be, w_t[j]), 0.0)
            n_t = jnp.where(same_head, _dot_tn(be, u0[j]) + _dot_tn(ke, vp), 0.0)
            st = state_ref[p]
            ys.append(_dot(ri[j] + pick(q_rows, j), st) + pick(y_rows, j))
            state_ref[p] = p_end[sl, :] * st + (_dot(m_t, st) + n_t)
    y = jnp.concatenate(ys, axis=1)

    gather, spread = _head_sum_matrices(RW_WIDTH, RW_HEAD_DIM)
    mean = _per_head_sum(y, gather, spread) * (1.0 / RW_HEAD_DIM)
    yc = y - mean
    var = _per_head_sum(yc * yc, gather, spread) * (1.0 / RW_HEAD_DIM)
    y = yc * lax.rsqrt(var + RW_GN_EPS) * lnx_ref[0:1, :] + lnx_ref[1:2, :]
    o_ref[...] = ((y + bonus_ref[...]) * g_ref[...]).astype(o_ref.dtype)


def _rwkv_chunks(parts, lnx, b, s):
    c = min(RW_CHUNK, s)
    nc = s // c
    row = pl.BlockSpec((c, RW_WIDTH), lambda bi, i: (bi * nc + i, 0))
    return pl.pallas_call(
        functools.partial(_rwkv_chunk_kernel, c=c),
        grid=(b, nc),
        in_specs=[row] * 8 + [pl.BlockSpec(lnx.shape, lambda bi, i: (0, 0))],
        out_specs=row,
        out_shape=jax.ShapeDtypeStruct((b * s, RW_WIDTH), BF16),
        scratch_shapes=[pltpu.VMEM((RW_WIDTH // LANES, LANES, LANES), F32)],
        compiler_params=_params("arbitrary", "arbitrary"),
        name="rwkv_chunks",
    )(*parts, lnx)


def _pad_rows(w, start, total):
    return jnp.zeros((total, w.shape[1]), w.dtype).at[start:start + w.shape[0]].set(w)


def _stack_rows(vectors, rows=8):
    mat = jnp.stack([v.reshape(-1).astype(F32) for v in vectors])
    return jnp.pad(mat, ((0, rows - mat.shape[0]), (0, 0)))


def _even_mixer(h_bf, v_first, b, s, w_in, mu, w_up, w0, a_up, a0, v_up, v0, g_up,
                k_k, k_a, r_k, lnx_g, lnx_b, w_out):
    n = b * s
    n_a = 3 * SB_WIDTH
    n_b = w_in.shape[1] - n_a
    qkv = _matmul(h_bf, w_in, BF16, m=n_a)
    w_b = jnp.pad(w_in[:, n_a:], ((0, 0), (0, RW_COLS - n_b)))
    pb = _matmul(h_bf, w_b, F32, tn=RW_COLS // 4)
    y_a = _stick_breaking(qkv.reshape(b, s, n_a), b, s).reshape(n, SB_WIDTH)

    o_w, o_a, o_g, o_v = 0, RW_DECAY_RANK, RW_DECAY_RANK + RW_ICLR_RANK, RW_DECAY_RANK + RW_ICLR_RANK + RW_GATE_RANK
    wup = _pad_rows(w_up, o_w, RW_LORA).astype(BF16)
    aup = _pad_rows(a_up, o_a, RW_LORA).astype(BF16)
    gup = _pad_rows(g_up, o_g, RW_LORA).astype(BF16)
    if v_up is None:
        vup = jnp.zeros((RW_LORA, RW_WIDTH), BF16)
        v0 = jnp.zeros((RW_WIDTH,), F32)
    else:
        vup = _pad_rows(v_up, o_v, RW_LORA).astype(BF16)
    mu_p = jnp.pad(mu, (0, RW_COLS - n_b)).reshape(1, RW_COLS)
    vecs = _stack_rows([w0, a0, v0, k_k, k_a, r_k])
    parts = _rwkv_prep(pb, b, s, mu_p, wup, aup, gup, vup, vecs, v_first)
    if v_first is None:
        v_first = parts[2]
    y_b = _rwkv_chunks(parts, _stack_rows([lnx_g, lnx_b]), b, s)
    y = jnp.concatenate([y_a, y_b], axis=1)
    return _matmul(y, w_out, F32), v_first


GDN_QKV = 2 * GDN_KEY_WIDTH + GDN_VAL_WIDTH
GDN_TS = 256
GDN_CHUNK = 64
GDN_HB = 16
GDN_STACK = 4
GDN_GROUPS = GDN_V_HEADS // GDN_HB
GDN_G_LANE = 64


def _gdn_prep_kernel(p_ref, s_ref, conv_ref, vec_ref, q_ref, k_ref, v_ref, bg_ref, last_ref, *, ts):
    @pl.when(pl.program_id(1) == 0)
    def _():
        last_ref[...] = jnp.zeros_like(last_ref)

    x = p_ref[...]
    last = last_ref[...]
    row8 = lax.broadcasted_iota(jnp.int32, (8, GDN_QKV), 0)
    acc = x * conv_ref[GDN_CONV - 1:GDN_CONV, :]
    for d in range(1, GDN_CONV):
        sh = pltpu.roll(x, d, 0)
        top = jnp.where(row8 < d, pltpu.roll(last, d, 0), sh[:8])
        sh = jnp.concatenate([top, sh[8:]], axis=0)
        acc = acc + sh * conv_ref[GDN_CONV - 1 - d:GDN_CONV - d, :]
    last_ref[...] = x[ts - 8:, :]
    y = _silu(acc)

    kw = GDN_KEY_WIDTH
    for h in range(GDN_QK_HEADS):
        sl = slice(h * GDN_HEAD_DIM, (h + 1) * GDN_HEAD_DIM)
        qh = y[:, sl]
        q_ref[:, sl] = qh * (lax.rsqrt(jnp.sum(qh * qh, axis=-1, keepdims=True) + L2_EPS) * GDN_HEAD_DIM ** -0.5)
        kh = y[:, kw + h * GDN_HEAD_DIM:kw + (h + 1) * GDN_HEAD_DIM]
        k_ref[:, sl] = kh * lax.rsqrt(jnp.sum(kh * kh, axis=-1, keepdims=True) + L2_EPS)
    v_ref[...] = y[:, 2 * kw:]

    small = s_ref[...]
    lane = lax.broadcasted_iota(jnp.int32, small.shape, 1)
    neg_a = vec_ref[0:1, :]
    dt_b = vec_ref[1:2, :]
    vals = jnp.where(lane < GDN_V_HEADS, _sigmoid(small), neg_a * _softplus(small + dt_b))
    src = lax.broadcasted_iota(jnp.int32, (LANES, LANES), 0)
    dst = lax.broadcasted_iota(jnp.int32, (LANES, LANES), 1)
    for grp in range(GDN_GROUPS):
        want = jnp.where(dst < GDN_G_LANE, grp * GDN_HB + dst, GDN_V_HEADS + grp * GDN_HB + dst - GDN_G_LANE)
        sel = jnp.where((src == want) & ((dst % GDN_G_LANE) < GDN_HB), 1.0, 0.0).astype(BF16)
        bg_ref[grp] = _dot_sel_rhs(vals, sel)


def _gdn_prep(p_main, p_small, conv_w, a_log, dt_bias, b, s):
    n = b * s
    ts = min(GDN_TS, s)
    nt = s // ts
    neg_a = jnp.zeros((LANES,), F32).at[GDN_V_HEADS:2 * GDN_V_HEADS].set(-jnp.exp(a_log.astype(F32)))
    dt_b = jnp.zeros((LANES,), F32).at[GDN_V_HEADS:2 * GDN_V_HEADS].set(dt_bias.astype(F32))
    vecs = _stack_rows([neg_a, dt_b])
    conv8 = jnp.pad(conv_w, ((0, 8 - GDN_CONV), (0, 0)))
    row = lambda c: pl.BlockSpec((ts, c), lambda bi, i: (bi * nt + i, 0))
    full = lambda a: pl.BlockSpec(a.shape, lambda bi, i: (0, 0))
    return pl.pallas_call(
        functools.partial(_gdn_prep_kernel, ts=ts),
        grid=(b, nt),
        in_specs=[row(GDN_QKV), row(LANES), full(conv8), full(vecs)],
        out_specs=[row(GDN_KEY_WIDTH), row(GDN_KEY_WIDTH), row(GDN_VAL_WIDTH),
                   pl.BlockSpec((GDN_GROUPS, ts, LANES), lambda bi, i: (0, bi * nt + i, 0))],
        out_shape=[jax.ShapeDtypeStruct((n, GDN_KEY_WIDTH), F32), jax.ShapeDtypeStruct((n, GDN_KEY_WIDTH), F32),
                   jax.ShapeDtypeStruct((n, GDN_VAL_WIDTH), F32), jax.ShapeDtypeStruct((GDN_GROUPS, n, LANES), F32)],
        scratch_shapes=[pltpu.VMEM((8, GDN_QKV), F32)],
        compiler_params=_params("arbitrary", "arbitrary"),
        name="gdn_prep",
    )(p_main, p_small, conv8, vecs)


def _gdn_chunk_kernel(q_ref, k_ref, v_ref, z_ref, bg_ref, ng_ref, o_ref, state_ref, *, c):
    @pl.when(pl.program_id(2) == 0)
    def _():
        state_ref[...] = jnp.zeros_like(state_ref)

    lower = jnp.where(lax.broadcasted_iota(jnp.int32, (c, c), 0)
                      >= lax.broadcasted_iota(jnp.int32, (c, c), 1), 1.0, 0.0).astype(BF16)
    bg = bg_ref[0]
    cum = _dot_sel_lhs(lower, bg)
    upper = jnp.where(lax.broadcasted_iota(jnp.int32, (c, c), 0)
                      <= lax.broadcasted_iota(jnp.int32, (c, c), 1), 1.0, 0.0).astype(BF16)
    cum_t = sum(lax.dot_general(piece, upper, (((0,), (0,)), ((), ())), preferred_element_type=F32)
                for piece in _split3(bg))

    rows = GDN_STACK * c
    rr = lax.broadcasted_iota(jnp.int32, (rows, rows), 0)
    cc = lax.broadcasted_iota(jnp.int32, (rows, rows), 1)
    same = (rr // c) == (cc // c)
    strict = same & (rr > cc)
    incl = same & (rr >= cc)
    spread_cols = jnp.where(lax.broadcasted_iota(jnp.int32, (c, rows), 0)
                            == lax.broadcasted_iota(jnp.int32, (c, rows), 1) % c, 1.0, 0.0).astype(BF16)
    stack = lambda parts: jnp.concatenate(parts, axis=0)

    for grp in range(GDN_HB // GDN_STACK):
        heads = range(grp * GDN_STACK, (grp + 1) * GDN_STACK)
        qk = lambda i: slice((i // 2) * GDN_HEAD_DIM, (i // 2 + 1) * GDN_HEAD_DIM)
        vs = lambda i: slice(i * GDN_HEAD_DIM, (i + 1) * GDN_HEAD_DIM)
        q = stack([q_ref[:, qk(i)] for i in heads])
        k = stack([k_ref[:, qk(i)] for i in heads])
        v = stack([v_ref[:, vs(i)] for i in heads])
        beta = stack([bg[:, i:i + 1] for i in heads])
        g_col = stack([cum[:, GDN_G_LANE + i:GDN_G_LANE + i + 1] for i in heads])
        g_end = stack([jnp.broadcast_to(cum[c - 1:c, GDN_G_LANE + i:GDN_G_LANE + i + 1], (c, 1)) for i in heads])
        g_own = stack([jnp.broadcast_to(cum_t[GDN_G_LANE + i:GDN_G_LANE + i + 1, :], (c, c)) for i in heads])
        d_own = jnp.exp(jnp.minimum(g_col - g_own, 0.0))
        decay = jnp.where(incl, _dot(d_own, spread_cols), 0.0)
        kb = k * beta
        low = jnp.where(strict, _dot_nt(kb, k) * decay, 0.0)
        t_inv = _neumann_inverse(-low, rows, nilpotent=c)
        e_col = jnp.exp(g_col)
        u = _dot(t_inv, v * beta)
        w = _dot(t_inv, kb * e_col)
        attn = _dot_nt(q, k) * decay
        k_dec = k * jnp.exp(g_end - g_col)
        q_hat = q * e_col - _dot(attn, w)
        o_in = _dot(attn, u)
        for j, i in enumerate(heads):
            rs = slice(j * c, (j + 1) * c)
            ktw = _dot_tn(k_dec[rs], w[rs])
            nn = _dot_tn(k_dec[rs], u[rs])
            st = state_ref[i]
            o = _dot(q_hat[rs], st) + o_in[rs]
            state_ref[i] = jnp.exp(g_end[rs][0:1, :]) * st + (nn - _dot(ktw, st))
            o = o * lax.rsqrt(jnp.mean(o * o, axis=-1, keepdims=True) + GDN_NORM_EPS) * ng_ref[0:1, :]
            o_ref[:, vs(i)] = (o * _silu(z_ref[:, vs(i)])).astype(o_ref.dtype)


def _gdn_chunks(q, k, v, p_main, bg, norm_g, b, s):
    c = min(GDN_CHUNK, s)
    nc = s // c
    kcols = GDN_HB // 2 * GDN_HEAD_DIM
    vcols = GDN_HB * GDN_HEAD_DIM
    z_blk0 = GDN_QKV // vcols
    ng = _stack_rows([norm_g])
    return pl.pallas_call(
        functools.partial(_gdn_chunk_kernel, c=c),
        grid=(b, GDN_GROUPS, nc),
        in_specs=[pl.BlockSpec((c, kcols), lambda bi, g, i: (bi * nc + i, g)),
                  pl.BlockSpec((c, kcols), lambda bi, g, i: (bi * nc + i, g)),
                  pl.BlockSpec((c, vcols), lambda bi, g, i: (bi * nc + i, g)),
                  pl.BlockSpec((c, vcols), lambda bi, g, i: (bi * nc + i, z_blk0 + g)),
                  pl.BlockSpec((1, c, LANES), lambda bi, g, i: (g, bi * nc + i, 0)),
                  pl.BlockSpec(ng.shape, lambda bi, g, i: (0, 0))],
        out_specs=pl.BlockSpec((c, vcols), lambda bi, g, i: (bi * nc + i, g)),
        out_shape=jax.ShapeDtypeStruct((b * s, GDN_VAL_WIDTH), BF16),
        scratch_shapes=[pltpu.VMEM((GDN_HB, GDN_HEAD_DIM, GDN_HEAD_DIM), F32)],
        compiler_params=_params("arbitrary", "arbitrary", "arbitrary"),
        name="gdn_chunks",
    )(q, k, v, p_main, bg, ng)


def _odd_mixer(h_bf, b, s, w_in, conv_w, a_log, dt_bias, norm_g, w_out):
    n_main = GDN_QKV + GDN_VAL_WIDTH
    p_main = _matmul(h_bf, w_in, F32, m=n_main)
    w_small = jnp.pad(w_in[:, n_main:], ((0, 0), (0, LANES - 2 * GDN_V_HEADS)))
    p_small = _matmul(h_bf, w_small, F32)
    q, k, v, bg = _gdn_prep(p_main, p_small, conv_w, a_log, dt_bias, b, s)
    o = _gdn_chunks(q, k, v, p_main, bg, norm_g, b, s)
    return _matmul(o, w_out, F32, tn=512)


ROUTER_TS = 512
GROUP_SIZE = N_EXPERTS // N_GROUPS


def _beats(other, mine, other_idx, my_idx):
    return jnp.where((other > mine) | ((other == mine) & (other_idx < my_idx)), 1, 0)


def _router_kernel(h_ref, rt_ref, bias_ref, eid_ref, pos_ref, w_ref, cnt_ref, run_ref, *, ts):
    @pl.when(pl.program_id(0) == 0)
    def _():
        run_ref[...] = jnp.zeros_like(run_ref)

    logits = lax.dot_general(rt_ref[...], h_ref[...], (((1,), (1,)), ((), ())),
                             precision=HI, preferred_element_type=F32)
    scores = _sigmoid(logits)
    choice = scores + jnp.tile(bias_ref[...], (1, ts // LANES))
    neg_inf = jnp.float32(-jnp.inf)

    grouped = choice.reshape(N_GROUPS, GROUP_SIZE, ts)
    m1 = jnp.max(grouped, axis=1, keepdims=True)
    ties = jnp.sum(jnp.where(grouped == m1, 1, 0), axis=1, keepdims=True)
    m2 = jnp.max(jnp.where(grouped < m1, grouped, neg_inf), axis=1, keepdims=True)
    group_score = (m1 + jnp.where(ties >= 2, m1, m2)).reshape(N_GROUPS, ts)

    g_idx = lax.broadcasted_iota(jnp.int32, (N_GROUPS, ts), 0)
    g_rank = jnp.zeros((N_GROUPS, ts), jnp.int32)
    for g in range(N_GROUPS):
        g_rank = g_rank + _beats(group_score[g:g + 1, :], group_score, g, g_idx)
    g_keep = jnp.where(g_rank < TOPK_GROUPS, 1.0, 0.0).reshape(N_GROUPS, 1, ts)
    e_keep = jnp.broadcast_to(g_keep, (N_GROUPS, GROUP_SIZE, ts)).reshape(N_EXPERTS, ts)
    masked = jnp.where(e_keep > 0.5, choice, neg_inf)

    e_idx = lax.broadcasted_iota(jnp.int32, (N_EXPERTS, ts), 0)
    e_rank = jnp.zeros((N_EXPERTS, ts), jnp.int32)
    for e in range(N_EXPERTS):
        e_rank = e_rank + _beats(masked[e:e + 1, :], masked, e, e_idx)
    chosen = e_rank < TOP_K
    top_w = jnp.where(chosen, scores, 0.0)
    gates = top_w / jnp.sum(top_w, axis=0, keepdims=True) * ROUTED_SCALE

    picks = jnp.where(chosen, 1.0, 0.0).astype(BF16)
    earlier = jnp.where(lax.broadcasted_iota(jnp.int32, (ts, ts), 0)
                        < lax.broadcasted_iota(jnp.int32, (ts, ts), 1), 1.0, 0.0).astype(BF16)
    pos = jnp.dot(picks, earlier, preferred_element_type=F32) + jnp.tile(run_ref[...], (1, ts // LANES))
    run_ref[...] += jnp.dot(picks, jnp.ones((ts, LANES), BF16), preferred_element_type=F32)
    cnt_ref[...] = run_ref[...]

    e_f = e_idx.astype(F32)
    ids, poss, ws = [], [], []
    for k in range(TOP_K):
        hit = e_rank == k
        ids.append(jnp.sum(jnp.where(hit, e_f, 0.0), axis=0, keepdims=True))
        poss.append(jnp.sum(jnp.where(hit, pos, 0.0), axis=0, keepdims=True))
        ws.append(jnp.sum(jnp.where(hit, gates, 0.0), axis=0, keepdims=True))
    eid_ref[...] = jnp.concatenate(ids, axis=0).astype(jnp.int32)
    pos_ref[...] = jnp.concatenate(poss, axis=0).astype(jnp.int32)
    w_ref[...] = jnp.concatenate(ws, axis=0)


def _router(h, router, router_bias):
    n = h.shape[0]
    ts = min(ROUTER_TS, n)
    rt = router.T.astype(F32)
    bias = jnp.broadcast_to(router_bias.astype(F32)[:, None], (N_EXPERTS, LANES))
    per_tok = pl.BlockSpec((TOP_K, ts), lambda i: (0, i))
    eid, pos, w, cnt = pl.pallas_call(
        functools.partial(_router_kernel, ts=ts),
        grid=(n // ts,),
        in_specs=[pl.BlockSpec((ts, D_MODEL), lambda i: (i, 0)),
                  pl.BlockSpec(rt.shape, lambda i: (0, 0)),
                  pl.BlockSpec(bias.shape, lambda i: (0, 0))],
        out_specs=[per_tok, per_tok, per_tok, pl.BlockSpec((N_EXPERTS, LANES), lambda i: (0, 0))],
        out_shape=[jax.ShapeDtypeStruct((TOP_K, n), jnp.int32), jax.ShapeDtypeStruct((TOP_K, n), jnp.int32),
                   jax.ShapeDtypeStruct((TOP_K, n), F32), jax.ShapeDtypeStruct((N_EXPERTS, LANES), F32)],
        scratch_shapes=[pltpu.VMEM((N_EXPERTS, LANES), F32)],
        compiler_params=_params("arbitrary"),
        name="router",
    )(h, rt, bias)
    return eid, pos, w, cnt[:, 0].astype(jnp.int32)


MOE_TM = 256
MOE_TS = 256
DMA_QUEUES = 2


def _swiglu_hidden(x, w_gu, ff):
    gu = jnp.dot(x, w_gu, preferred_element_type=F32)
    return _silu(gu[:, :ff]) * gu[:, ff:]


def _tile_copy(src, src_row, dst, dst_row, sem):
    return pltpu.make_async_copy(src.at[pl.ds(pl.multiple_of(src_row, PACK_ROWS), PACK_ROWS), :],
                                 dst.at[pl.ds(pl.multiple_of(dst_row, PACK_ROWS), PACK_ROWS), :], sem)


def _dispatch_kernel(slots_ref, hp_ref, xs_in_ref, xs_ref, sem, *, ts):
    del xs_in_ref

    def issue(t, carry):
        for k in range(TOP_K):
            slot = slots_ref[0, 0, t * TOP_K + k]
            _tile_copy(hp_ref, t * PACK_ROWS, xs_ref, slot * PACK_ROWS, sem).start(priority=k % DMA_QUEUES)
        return carry

    lax.fori_loop(0, ts, issue, 0)
    for _ in range(TOP_K):
        pltpu.make_async_copy(hp_ref, xs_ref.at[pl.ds(0, ts * PACK_ROWS), :], sem).wait()


def _dispatch(hp, slots, n_slots):
    n = hp.shape[0] // PACK_ROWS
    ts = min(MOE_TS, n)
    xs0 = jnp.zeros((n_slots * PACK_ROWS, LANES), jnp.uint32)
    return pl.pallas_call(
        functools.partial(_dispatch_kernel, ts=ts),
        grid=(n // ts,),
        in_specs=[pl.BlockSpec((1, 1, ts * TOP_K), lambda i: (i, 0, 0), memory_space=pltpu.SMEM),
                  pl.BlockSpec((ts * PACK_ROWS, LANES), lambda i: (i, 0)),
                  pl.BlockSpec(memory_space=pl.ANY)],
        out_specs=pl.BlockSpec(memory_space=pl.ANY),
        out_shape=jax.ShapeDtypeStruct(xs0.shape, jnp.uint32),
        scratch_shapes=[pltpu.SemaphoreType.DMA(())],
        input_output_aliases={2: 0},
        compiler_params=_params("arbitrary"),
        name="moe_dispatch",
    )(slots, hp, xs0)


def _experts_kernel(te_ref, nv_ref, xs_ref, wg_ref, wu_ref, wd_ref, ys_ref, gu_ref, dn_ref):
    t = pl.program_id(0)
    live = t < nv_ref[0]
    new_expert = (t == 0) | (te_ref[t] != te_ref[jnp.maximum(t - 1, 0)])

    @pl.when(live & new_expert)
    def _():
        gu_ref[:, :EXPERT_FF] = wg_ref[0].astype(BF16)
        gu_ref[:, EXPERT_FF:] = wu_ref[0].astype(BF16)
        dn_ref[...] = wd_ref[0].astype(BF16)

    @pl.when(live)
    def _():
        x = _load_packed(xs_ref, 0, MOE_TM)
        hid = _swiglu_hidden(x, gu_ref[...], EXPERT_FF)
        y = jnp.dot(hid.astype(BF16), dn_ref[...], preferred_element_type=F32)
        _store_packed(ys_ref, y, MOE_TM)

    @pl.when(jnp.logical_not(live))
    def _():
        ys_ref[...] = jnp.zeros_like(ys_ref)


def _experts(xs, tile_expert, n_valid, e_gate, e_up, e_down):
    n_tiles = tile_expert.shape[0]
    rows = MOE_TM * PACK_ROWS
    weight = lambda w: pl.BlockSpec((1,) + w.shape[1:], lambda t, te, nv: (te[t], 0, 0))
    return pl.pallas_call(
        _experts_kernel,
        grid_spec=pltpu.PrefetchScalarGridSpec(
            num_scalar_prefetch=2,
            grid=(n_tiles,),
            in_specs=[pl.BlockSpec((rows, LANES), lambda t, te, nv: (t, 0)),
                      weight(e_gate), weight(e_up), weight(e_down)],
            out_specs=pl.BlockSpec((rows, LANES), lambda t, te, nv: (t, 0)),
            scratch_shapes=[pltpu.VMEM((D_MODEL, 2 * EXPERT_FF), BF16), pltpu.VMEM((EXPERT_FF, D_MODEL), BF16)]),
        out_shape=jax.ShapeDtypeStruct(xs.shape, jnp.uint32),
        compiler_params=_params("arbitrary"),
        name="moe_experts",
    )(tile_expert, n_valid, xs, e_gate, e_up, e_down)


def _combine_kernel(slots_ref, w_ref, hb_ref, h_ref, ys_ref, sgu_ref, sdn_ref, g_ref, b_ref,
                    o_ref, ob_ref, ybuf, sem, *, ts):
    def issue(t, carry):
        for k in range(TOP_K):
            slot = slots_ref[0, 0, t * TOP_K + k]
            _tile_copy(ys_ref, slot * PACK_ROWS, ybuf, (k * ts + t) * PACK_ROWS, sem).start(priority=k % DMA_QUEUES)
        return carry

    lax.fori_loop(0, ts, issue, 0)
    hid = _swiglu_hidden(hb_ref[...], sgu_ref[...], SHARED_FF)
    y = jnp.dot(hid.astype(BF16), sdn_ref[...], preferred_element_type=F32)
    for _ in range(TOP_K):
        pltpu.make_async_copy(ys_ref.at[pl.ds(0, ts * PACK_ROWS), :], ybuf.at[pl.ds(0, ts * PACK_ROWS), :], sem).wait()
    for k in range(TOP_K):
        y = y + w_ref[:, k:k + 1] * _load_packed(ybuf, k * ts * PACK_ROWS, ts).astype(F32)
    out = _ln_rows(DN_ALPHA * h_ref[...] + y, g_ref[...], b_ref[...])
    o_ref[...] = out
    ob_ref[...] = out.astype(BF16)


def _combine(h, h_bf, ys, slots, w_tok, s_gu, s_dn, ln_g, ln_b):
    n, d = h.shape
    ts = min(MOE_TS, n)
    row = lambda c: pl.BlockSpec((ts, c), lambda i: (i, 0))
    full = lambda a: pl.BlockSpec(a.shape, lambda i: (0,) * a.ndim)
    g2, b2 = ln_g.reshape(1, d), ln_b.reshape(1, d)
    return pl.pallas_call(
        functools.partial(_combine_kernel, ts=ts),
        grid=(n // ts,),
        in_specs=[pl.BlockSpec((1, 1, ts * TOP_K), lambda i: (i, 0, 0), memory_space=pltpu.SMEM),
                  row(TOP_K), row(d), row(d), pl.BlockSpec(memory_space=pl.ANY),
                  full(s_gu), full(s_dn), full(g2), full(b2)],
        out_specs=[row(d), row(d)],
        out_shape=[jax.ShapeDtypeStruct((n, d), F32), jax.ShapeDtypeStruct((n, d), BF16)],
        scratch_shapes=[pltpu.VMEM((TOP_K * ts * PACK_ROWS, LANES), jnp.uint32), pltpu.SemaphoreType.DMA(())],
        compiler_params=_params("arbitrary"),
        name="moe_combine",
    )(slots, w_tok, h_bf, h, ys, s_gu, s_dn, g2, b2)


def _ffn(h, h_bf, hp, router, router_bias, e_gate, e_up, e_down, s_gate, s_up, s_down, ln_g, ln_b):
    n = h.shape[0]
    eid, pos, w, counts = _router(h, router, router_bias)
    padded = (counts + MOE_TM - 1) // MOE_TM * MOE_TM
    ends = jnp.cumsum(padded)
    starts = ends - padded
    n_slots = n * TOP_K + N_EXPERTS * MOE_TM
    n_tiles = n_slots // MOE_TM
    ts = min(MOE_TS, n)
    experts = jnp.arange(N_EXPERTS, dtype=jnp.int32)
    start_of = jnp.sum(jnp.where(eid[..., None] == experts, starts, 0), axis=-1)
    slots = (start_of + pos).T.reshape(n // ts, 1, ts * TOP_K)
    tile_start = jnp.arange(n_tiles, dtype=jnp.int32) * MOE_TM
    tile_expert = jnp.minimum(jnp.sum((ends[None, :] <= tile_start[:, None]).astype(jnp.int32), axis=1),
                              N_EXPERTS - 1)
    n_valid = (ends[-1:] // MOE_TM).astype(jnp.int32)

    s_gu = jnp.concatenate([s_gate, s_up], axis=1).astype(BF16)
    xs = _dispatch(hp, slots, n_slots)
    ys = _experts(xs, tile_expert, n_valid, e_gate, e_up, e_down)
    return _combine(h, h_bf, ys, slots, w.T, s_gu, s_down.astype(BF16), ln_g, ln_b)


def _trunk(x, layers):
    b, s, d = x.shape
    n = b * s
    x = x.reshape(n, d)
    x_bf = x.astype(BF16)
    v_first = None
    for i, (mixer, ffn) in enumerate(layers):
        if i % 2 == 0:
            mix, v_first = _even_mixer(x_bf, v_first, b, s, *mixer)
        else:
            mix = _odd_mixer(x_bf, b, s, *mixer)
        h, h_bf, hp = _add_ln(x, mix, ffn[0], ffn[1])
        x, x_bf = _ffn(h, h_bf, hp, *ffn[2:])
    return x.reshape(b, s, d)


def kernel(x, l0_w_in, l0_mu, l0_w_up, l0_w0, l0_a_up, l0_a0, l0_g_up, l0_k_k, l0_k_a, l0_r_k, l0_lnx_g, l0_lnx_b, l0_w_out, l0_ln1_g, l0_ln1_b, l0_router, l0_router_bias, l0_e_gate, l0_e_up, l0_e_down, l0_s_gate, l0_s_up, l0_s_down, l0_ln2_g, l0_ln2_b, l1_w_in, l1_conv, l1_a_log, l1_dt_bias, l1_norm_g, l1_w_out, l1_ln1_g, l1_ln1_b, l1_router, l1_router_bias, l1_e_gate, l1_e_up, l1_e_down, l1_s_gate, l1_s_up, l1_s_down, l1_ln2_g, l1_ln2_b, l2_w_in, l2_mu, l2_w_up, l2_w0, l2_a_up, l2_a0, l2_v_up, l2_v0, l2_g_up, l2_k_k, l2_k_a, l2_r_k, l2_lnx_g, l2_lnx_b, l2_w_out, l2_ln1_g, l2_ln1_b, l2_router, l2_router_bias, l2_e_gate, l2_e_up, l2_e_down, l2_s_gate, l2_s_up, l2_s_down, l2_ln2_g, l2_ln2_b, l3_w_in, l3_conv, l3_a_log, l3_dt_bias, l3_norm_g, l3_w_out, l3_ln1_g, l3_ln1_b, l3_router, l3_router_bias, l3_e_gate, l3_e_up, l3_e_down, l3_s_gate, l3_s_up, l3_s_down, l3_ln2_g, l3_ln2_b):
    layers = [
        ((l0_w_in, l0_mu, l0_w_up, l0_w0, l0_a_up, l0_a0, None, None, l0_g_up, l0_k_k, l0_k_a, l0_r_k,
          l0_lnx_g, l0_lnx_b, l0_w_out),
         (l0_ln1_g, l0_ln1_b, l0_router, l0_router_bias, l0_e_gate, l0_e_up, l0_e_down,
          l0_s_gate, l0_s_up, l0_s_down, l0_ln2_g, l0_ln2_b)),
        ((l1_w_in, l1_conv, l1_a_log, l1_dt_bias, l1_norm_g, l1_w_out),
         (l1_ln1_g, l1_ln1_b, l1_router, l1_router_bias, l1_e_gate, l1_e_up, l1_e_down,
          l1_s_gate, l1_s_up, l1_s_down, l1_ln2_g, l1_ln2_b)),
        ((l2_w_in, l2_mu, l2_w_up, l2_w0, l2_a_up, l2_a0, l2_v_up, l2_v0, l2_g_up, l2_k_k, l2_k_a, l2_r_k,
          l2_lnx_g, l2_lnx_b, l2_w_out),
         (l2_ln1_g, l2_ln1_b, l2_router, l2_router_bias, l2_e_gate, l2_e_up, l2_e_down,
          l2_s_gate, l2_s_up, l2_s_down, l2_ln2_g, l2_ln2_b)),
        ((l3_w_in, l3_conv, l3_a_log, l3_dt_bias, l3_norm_g, l3_w_out),
         (l3_ln1_g, l3_ln1_b, l3_router, l3_router_bias, l3_e_gate, l3_e_up, l3_e_down,
          l3_s_gate, l3_s_up, l3_s_down, l3_ln2_g, l3_ln2_b)),
    ]
    return _trunk(x, layers)
```
